```python
import math
import jax, jax.numpy as jnp
from jax import lax
import numpy as np

D_MODEL = 2048
BATCH = 1
SEQ = 8192
DEPTH = 4

HEAD_DIM = 64
SGU_WIDTH = D_MODEL // 2
N_SGU_HEADS = SGU_WIDTH // HEAD_DIM
SGU_HEAD_DIM = SGU_WIDTH // N_SGU_HEADS
CHUNK = 128
ATTN_WIDTH = D_MODEL - SGU_WIDTH
N_Q_HEADS = ATTN_WIDTH // HEAD_DIM
N_KV_HEADS = 4
GROUP = N_Q_HEADS // N_KV_HEADS
WINDOW = 128
BLOCK = 128
NUM_BUCKETS = 32
MAX_DISTANCE = 128
IN_WIDTH = 2 * SGU_WIDTH + N_Q_HEADS * HEAD_DIM + 2 * N_KV_HEADS * HEAD_DIM
D_FF = -(-8 * D_MODEL // (3 * 256)) * 256
EPS = 1e-6
NEG_INF = -1e30

kernel_name = "hybrid_sgu_swa_sink_trunk"


def rms_norm(x, g):
    xf = x.astype(jnp.float32)
    y = xf * lax.rsqrt(jnp.mean(xf * xf, axis=-1, keepdims=True) + EPS)
    return (y * g.astype(jnp.float32)).astype(x.dtype)


def t5_causal_bucket(dist):
    n = jnp.maximum(dist, 0)
    max_exact = NUM_BUCKETS // 2
    nf = jnp.maximum(n, 1).astype(jnp.float32)
    large = max_exact + (jnp.log(nf / max_exact) / math.log(MAX_DISTANCE / max_exact)
                         * (NUM_BUCKETS - max_exact)).astype(jnp.int32)
    large = jnp.minimum(large, NUM_BUCKETS - 1)
    return jnp.where(n < max_exact, n, large)


def chunked_sgu(z, v_norm_g, w_s, b_s):
    b, s, _ = z.shape
    u, v = jnp.split(z, 2, axis=-1)
    v = v.reshape(b, s // CHUNK, CHUNK, N_SGU_HEADS, SGU_HEAD_DIM)
    v = rms_norm(v, v_norm_g)
    w = w_s * jnp.tril(jnp.ones((CHUNK, CHUNK), w_s.dtype))
    gate = jnp.einsum('hts,bcshd->bcthd', w, v) + b_s.T[None, None, :, :, None]
    return u * gate.reshape(b, s, SGU_WIDTH)


def swa_sink_attention(q, k, v, q_norm_g, k_norm_g, sinks, rel_bias):
    b, s, _ = q.shape
    nb = s // BLOCK
    q = rms_norm(q.reshape(b, nb, BLOCK, N_KV_HEADS, GROUP, HEAD_DIM), q_norm_g)
    k = rms_norm(k.reshape(b, nb, BLOCK, N_KV_HEADS, HEAD_DIM), k_norm_g)
    v = v.reshape(b, nb, BLOCK, N_KV_HEADS, HEAD_DIM)

    def band(t):
        prev = jnp.concatenate([jnp.zeros_like(t[:, :1]), t[:, :-1]], axis=1)
        return jnp.concatenate([prev, t], axis=2)

    kb, vb = band(k), band(v)
    scale = 1.0 / math.sqrt(HEAD_DIM)
    scores = jnp.einsum('bnqkgd,bnskd->bnkgqs', q, kb).astype(jnp.float32) * scale

    qi = jnp.arange(BLOCK)[:, None]
    kj = jnp.arange(2 * BLOCK)[None, :]
    dist = qi + BLOCK - kj
    bias = rel_bias[t5_causal_bucket(dist)].astype(jnp.float32)
    bias = jnp.transpose(bias, (2, 0, 1)).reshape(N_KV_HEADS, GROUP, BLOCK, 2 * BLOCK)
    in_window = (dist >= 0) & (dist < WINDOW)
    key_pos = jnp.arange(nb)[:, None] * BLOCK - BLOCK + kj
    valid = in_window[None] & (key_pos >= 0)[:, None, :]
    scores = jnp.where(valid[None, :, None, None], scores + bias, NEG_INF)

    sink = jnp.broadcast_to(
        sinks.astype(jnp.float32).reshape(N_KV_HEADS, GROUP)[None, None, :, :, None, None],
        scores.shape[:-1] + (1,))
    probs = jax.nn.softmax(jnp.concatenate([scores, sink], axis=-1), axis=-1)[..., :-1]
    out = jnp.einsum('bnkgqs,bnskd->bnqkgd', probs.astype(vb.dtype), vb)
    return out.reshape(b, s, ATTN_WIDTH)


def setup_inputs(seed: int = 0) -> dict:
    key = jax.random.key(seed)
    ks = jax.random.split(key, 20)
    f32 = jnp.float32
    nrm = lambda k, shape, sc: jax.random.normal(k, shape, f32) * sc
    gain = lambda k, shape: 1.0 + 0.02 * jax.random.normal(k, shape, f32)
    return {
        "x": jax.random.normal(ks[0], (BATCH, SEQ, D_MODEL), f32),
        "rel_bias": nrm(ks[1], (NUM_BUCKETS, N_Q_HEADS), 0.1),
        "norm1_g": gain(ks[2], (DEPTH, D_MODEL)),
        "w_in": nrm(ks[3], (DEPTH, D_MODEL, IN_WIDTH), D_MODEL ** -0.5),
        "sgu_norm_g": gain(ks[4], (DEPTH, N_SGU_HEADS, SGU_HEAD_DIM)),
        "sgu_w": nrm(ks[5], (DEPTH, N_SGU_HEADS, CHUNK, CHUNK), CHUNK ** -0.5),
        "sgu_b": gain(ks[6], (DEPTH, N_SGU_HEADS, CHUNK)),
        "q_norm_g": gain(ks[7], (DEPTH, HEAD_DIM)),
        "k_norm_g": gain(ks[8], (DEPTH, HEAD_DIM)),
        "sinks": nrm(ks[9], (DEPTH, N_Q_HEADS), 0.5),
        "out_norm_a": gain(ks[10], (DEPTH, SGU_WIDTH)),
        "out_norm_b": gain(ks[11], (DEPTH, ATTN_WIDTH)),
        "w_out": nrm(ks[12], (DEPTH, D_MODEL, D_MODEL), D_MODEL ** -0.5),
        "norm2_g": gain(ks[13], (DEPTH, D_MODEL)),
        "w_gate": nrm(ks[14], (DEPTH, D_MODEL, D_FF), D_MODEL ** -0.5),
        "w_up": nrm(ks[15], (DEPTH, D_MODEL, D_FF), D_MODEL ** -0.5),
        "w_down": nrm(ks[16], (DEPTH, D_FF, D_MODEL), D_FF ** -0.5),
    }


def reference(x, rel_bias, norm1_g, w_in, sgu_norm_g, sgu_w, sgu_b, q_norm_g, k_norm_g,
              sinks, out_norm_a, out_norm_b, w_out, norm2_g, w_gate, w_up, w_down):
    q_end = 2 * SGU_WIDTH + N_Q_HEADS * HEAD_DIM
    k_end = q_end + N_KV_HEADS * HEAD_DIM
    for l in range(DEPTH):
        h = rms_norm(x, norm1_g[l])
        z = h @ w_in[l]
        z_sgu = jax.nn.gelu(z[..., :2 * SGU_WIDTH], approximate=False)
        out_a = chunked_sgu(z_sgu, sgu_norm_g[l], sgu_w[l], sgu_b[l])
        out_b = swa_sink_attention(z[..., 2 * SGU_WIDTH:q_end], z[..., q_end:k_end],
                                   z[..., k_end:], q_norm_g[l], k_norm_g[l], sinks[l],
                                   rel_bias)
        mixed = jnp.concatenate([rms_norm(out_a, out_norm_a[l]),
                                 rms_norm(out_b, out_norm_b[l])], axis=-1)
        x = x + mixed @ w_out[l]
        h2 = rms_norm(x, norm2_g[l])
        x = x + (jax.nn.silu(h2 @ w_gate[l]) * (h2 @ w_up[l])) @ w_down[l]
    return x
```

```python
import functools
import math

import numpy as np
import jax
import jax.numpy as jnp
from jax import lax
from jax.experimental import pallas as pl
from jax.experimental.pallas import tpu as pltpu

D_MODEL = 2048
SEQ = 8192
DEPTH = 4
HEAD_DIM = 64
SGU_WIDTH = D_MODEL // 2
N_SGU_HEADS = SGU_WIDTH // HEAD_DIM
CHUNK = 128
ATTN_WIDTH = D_MODEL - SGU_WIDTH
N_Q_HEADS = ATTN_WIDTH // HEAD_DIM
N_KV_HEADS = 4
GROUP = N_Q_HEADS // N_KV_HEADS
KV_WIDTH = N_KV_HEADS * HEAD_DIM
WINDOW = 128
BLOCK = 128
NUM_BUCKETS = 32
MAX_DISTANCE = 128
IN_WIDTH = 2 * SGU_WIDTH + ATTN_WIDTH + 2 * KV_WIDTH
D_FF = -(-8 * D_MODEL // (3 * 256)) * 256
EPS = 1e-6
NEG_INF = -1e30
SCALE = 1.0 / math.sqrt(HEAD_DIM)

LANES = 128
HEADS_PER_LANE_BLOCK = LANES // HEAD_DIM
MIB = 1024 * 1024

F32 = jnp.float32
BF16 = jnp.bfloat16

IN_TM = 1024
IN_TN = 512
N_GELU_TILES = (2 * SGU_WIDTH) // IN_TN
FFN_TM = 512
FFN_TF = 512


def _rms_rows(xf, g):
    ms = jnp.mean(xf * xf, axis=-1, keepdims=True)
    return xf * lax.rsqrt(ms + EPS) * g


def _split3(x):
    t1 = x.astype(BF16)
    r = x - t1.astype(F32)
    t2 = r.astype(BF16)
    r = r - t2.astype(F32)
    return t1, t2, r.astype(BF16)


def _dot(a, b):
    return jnp.dot(a, b, preferred_element_type=F32)


def _head_rsqrt(x, seg_ref, exp_ref):
    seg = seg_ref[...]
    a1, a2, a3 = _split3(x * x)
    ss = _dot(a1, seg) + _dot(a2, seg) + _dot(a3, seg)
    r = lax.rsqrt(ss * (1.0 / HEAD_DIM) + EPS)
    r1, r2, r3 = _split3(r)
    ex = exp_ref[...]
    return (_dot(r1, ex) + _dot(r2, ex)) + _dot(r3, ex)


def _in_proj_kernel(x_ref, g_ref, w_ref, o_ref, h_ref):
    j = pl.program_id(1)

    @pl.when(j == 0)
    def _():
        h_ref[...] = _rms_rows(x_ref[...], g_ref[...]).astype(BF16)

    @pl.when(j < N_GELU_TILES)
    def _():
        z = _dot(h_ref[...], w_ref[...])
        o_ref[...] = 0.5 * z * (1.0 + lax.erf(z * (1.0 / math.sqrt(2.0))))

    @pl.when(j >= N_GELU_TILES)
    def _():
        o_ref[...] = _dot(h_ref[...], w_ref[...])


def _in_proj(x, g, w):
    vmem = (2 * IN_TM * D_MODEL * 4 + 2 * D_MODEL * IN_TN * 2
            + 4 * IN_TM * IN_TN * 4 + IN_TM * D_MODEL * 2 + 4 * MIB)
    return pl.pallas_call(
        _in_proj_kernel,
        out_shape=jax.ShapeDtypeStruct((SEQ, IN_WIDTH), F32),
        grid=(SEQ // IN_TM, IN_WIDTH // IN_TN),
        in_specs=[
            pl.BlockSpec((IN_TM, D_MODEL), lambda i, j: (i, 0)),
            pl.BlockSpec((1, D_MODEL), lambda i, j: (0, 0)),
            pl.BlockSpec((D_MODEL, IN_TN), lambda i, j: (0, j)),
        ],
        out_specs=pl.BlockSpec((IN_TM, IN_TN), lambda i, j: (i, j)),
        scratch_shapes=[pltpu.VMEM((IN_TM, D_MODEL), BF16)],
        compiler_params=pltpu.CompilerParams(
            dimension_semantics=("arbitrary", "arbitrary"),
            vmem_limit_bytes=vmem),
        name="in_proj",
    )(x, g, w)


def _mixer_kernel(sinks_ref, relb_ref,
                  zu_ref, zv_ref, zq_ref, zkv_ref,
                  sgu_g_ref, sgu_w_ref, sgu_b_ref, qg_ref, kg_ref,
                  bucket_ref, seg_ref, exp_ref, ga_ref, gb_ref,
                  o_ref,
                  wm_ref, bias_ref, kband_ref, vband_ref):
    i = pl.program_id(0)
    lane = lax.broadcasted_iota(jnp.int32, (BLOCK, LANES), 1)
    lo_half = lane < HEAD_DIM

    @pl.when(i == 0)
    def _():
        t_idx = lax.broadcasted_iota(jnp.int32, (CHUNK, CHUNK), 0)
        s_idx = lax.broadcasted_iota(jnp.int32, (CHUNK, CHUNK), 1)
        keep = s_idx <= t_idx

        def mask_head(h, c):
            wm_ref[h] = jnp.where(keep, sgu_w_ref[h], 0.0).astype(BF16)
            return c
        lax.fori_loop(0, N_SGU_HEADS, mask_head, 0)

        bucket = bucket_ref[...]

        def bias_head(h, c):
            acc = jnp.zeros((BLOCK, 2 * BLOCK), F32)
            for b in range(NUM_BUCKETS):
                acc = jnp.where(bucket == b, relb_ref[b, h], acc)
            bias_ref[h] = acc
            return c
        lax.fori_loop(0, N_Q_HEADS, bias_head, 0)

        kband_ref[...] = jnp.zeros(kband_ref.shape, BF16)
        vband_ref[...] = jnp.zeros(vband_ref.shape, BF16)

    v = zv_ref[...]
    vn = (v * _head_rsqrt(v, seg_ref, exp_ref) * sgu_g_ref[...]).astype(BF16)
    gates = []
    for p in range(N_SGU_HEADS // HEADS_PER_LANE_BLOCK):
        vb = vn[:, p * LANES:(p + 1) * LANES]
        g_lo = _dot(wm_ref[2 * p], vb)
        g_hi = _dot(wm_ref[2 * p + 1], vb)
        gates.append(jnp.where(lo_half, g_lo, g_hi))
    gate = jnp.concatenate(gates, axis=-1) + sgu_b_ref[...]
    out_a = zu_ref[...] * gate
    o_ref[:, :SGU_WIDTH] = _rms_rows(out_a, ga_ref[...]).astype(BF16)

    q = zq_ref[...]
    qn = (q * _head_rsqrt(q, seg_ref, exp_ref) * qg_ref[...] * SCALE).astype(BF16)
    kv = zkv_ref[...]
    k = kv[:, :KV_WIDTH]
    kn = k * _head_rsqrt(k, seg_ref.at[:KV_WIDTH], exp_ref.at[:, :KV_WIDTH]) * kg_ref[...]
    vv = kv[:, KV_WIDTH:]

    for j in range(N_KV_HEADS):
        p, half = divmod(j, HEADS_PER_LANE_BLOCK)
        kb = kn[:, p * LANES:(p + 1) * LANES]
        vb = vv[:, p * LANES:(p + 1) * LANES]
        keep = lo_half if half == 0 else jnp.logical_not(lo_half)
        k_same = jnp.where(keep, kb, 0.0)
        v_same = jnp.where(keep, vb, 0.0)
        k_other = pltpu.roll(k_same, HEAD_DIM, 1)
        v_other = pltpu.roll(v_same, HEAD_DIM, 1)
        kband_ref[j, half, BLOCK:, :] = k_same.astype(BF16)
        kband_ref[j, 1 - half, BLOCK:, :] = k_other.astype(BF16)
        vband_ref[j, half, BLOCK:, :] = v_same.astype(BF16)
        vband_ref[j, 1 - half, BLOCK:, :] = v_other.astype(BF16)

    qi = lax.broadcasted_iota(jnp.int32, (BLOCK, 2 * BLOCK), 0)
    kj = lax.broadcasted_iota(jnp.int32, (BLOCK, 2 * BLOCK), 1)
    dist = qi + BLOCK - kj
    first_key = jnp.where(i > 0, 0, BLOCK)
    valid = (dist >= 0) & (dist < WINDOW) & (kj >= first_key)

    outs = []
    for p in range(N_Q_HEADS // HEADS_PER_LANE_BLOCK):
        qb = qn[:, p * LANES:(p + 1) * LANES]
        j = (p * HEADS_PER_LANE_BLOCK) // GROUP
        acc = None
        for half in range(HEADS_PER_LANE_BLOCK):
            h = p * HEADS_PER_LANE_BLOCK + half
            s = lax.dot_general(qb, kband_ref[j, half], (((1,), (1,)), ((), ())),
                                preferred_element_type=F32)
            s = jnp.where(valid, s + bias_ref[h], NEG_INF)
            sink = sinks_ref[h]
            m = jnp.maximum(jnp.max(s, axis=-1, keepdims=True), sink)
            e = jnp.exp(s - m)
            den = jnp.sum(e, axis=-1, keepdims=True) + jnp.exp(sink - m)
            prob = (e * (1.0 / den)).astype(BF16)
            o = _dot(prob, vband_ref[j, half])
            acc = o if acc is None else acc + o
        outs.append(acc)
    out_b = jnp.concatenate(outs, axis=-1)
    o_ref[:, SGU_WIDTH:] = _rms_rows(out_b, gb_ref[...]).astype(BF16)

    kband_ref[:, :, :BLOCK, :] = kband_ref[:, :, BLOCK:, :]
    vband_ref[:, :, :BLOCK, :] = vband_ref[:, :, BLOCK:, :]


def _t5_bucket_table():
    qi = np.arange(BLOCK)[:, None]
    kj = np.arange(2 * BLOCK)[None, :]
    n = np.maximum(qi + BLOCK - kj, 0)
    max_exact = NUM_BUCKETS // 2
    nf = np.maximum(n, 1).astype(np.float32)
    large = max_exact + (np.log(nf / np.float32(max_exact))
                         / np.float32(math.log(MAX_DISTANCE / max_exact))
                         * np.float32(NUM_BUCKETS - max_exact)).astype(np.int32)
    large = np.minimum(large, NUM_BUCKETS - 1)
    return np.where(n < max_exact, n, large).astype(np.int32)


def _head_segment_matrices():
    lane = np.arange(SGU_WIDTH)
    seg = (lane[:, None] // HEAD_DIM == np.arange(LANES)[None, :])
    return seg.astype(np.float32), seg.T.astype(np.float32)


def _mixer(z, sinks, rel_bias, sgu_g, sgu_w, sgu_b, qg, kg, ga, gb):
    seg, exp = _head_segment_matrices()
    row = lambda n: pl.BlockSpec((1, n), lambda i: (0, 0))
    smem = pl.BlockSpec(memory_space=pltpu.SMEM)
    zcol = lambda w, c: pl.BlockSpec((BLOCK, w), lambda i: (i, c))
    return pl.pallas_call(
        _mixer_kernel,
        out_shape=jax.ShapeDtypeStruct((SEQ, D_MODEL), BF16),
        grid=(SEQ // BLOCK,),
        in_specs=[
            smem, smem,
            zcol(SGU_WIDTH, 0), zcol(SGU_WIDTH, 1), zcol(ATTN_WIDTH, 2),
            zcol(2 * KV_WIDTH, (2 * SGU_WIDTH + ATTN_WIDTH) // (2 * KV_WIDTH)),
            row(SGU_WIDTH),
            pl.BlockSpec((N_SGU_HEADS, CHUNK, CHUNK), lambda i: (0, 0, 0)),
            pl.BlockSpec((CHUNK, SGU_WIDTH), lambda i: (0, 0)),
            row(ATTN_WIDTH), row(KV_WIDTH),
            pl.BlockSpec((BLOCK, 2 * BLOCK), lambda i: (0, 0)),
            pl.BlockSpec((SGU_WIDTH, LANES), lambda i: (0, 0)),
            pl.BlockSpec((LANES, SGU_WIDTH), lambda i: (0, 0)),
            row(SGU_WIDTH), row(ATTN_WIDTH),
        ],
        out_specs=pl.BlockSpec((BLOCK, D_MODEL), lambda i: (i, 0)),
        scratch_shapes=[
            pltpu.VMEM((N_SGU_HEADS, CHUNK, CHUNK), BF16),
            pltpu.VMEM((N_Q_HEADS, BLOCK, 2 * BLOCK), F32),
            pltpu.VMEM((N_KV_HEADS, HEADS_PER_LANE_BLOCK, 2 * BLOCK, LANES), BF16),
            pltpu.VMEM((N_KV_HEADS, HEADS_PER_LANE_BLOCK, 2 * BLOCK, LANES), BF16),
        ],
        compiler_params=pltpu.CompilerParams(
            dimension_semantics=("arbitrary",),
            vmem_limit_bytes=32 * MIB),
        name="mixer",
    )(sinks, rel_bias, z, z, z, z, sgu_g, sgu_w, sgu_b, qg, kg,
      jnp.asarray(_t5_bucket_table()), jnp.asarray(seg, BF16), jnp.asarray(exp, BF16),
      ga, gb)


def _ffn_kernel(x_ref, mix_ref, wo_ref, g_ref, wg_ref, wu_ref, wd_ref, o_ref, h_ref):
    f = pl.program_id(1)

    @pl.when(f == 0)
    def _():
        x1 = x_ref[...] + _dot(mix_ref[...], wo_ref[...])
        o_ref[...] = x1
        h_ref[...] = _rms_rows(x1, g_ref[...]).astype(BF16)

    h = h_ref[...]
    gate = _dot(h, wg_ref[...])
    up = _dot(h, wu_ref[...])
    act = (gate * jax.nn.sigmoid(gate) * up).astype(BF16)
    o_ref[...] += _dot(act, wd_ref[...])


def _ffn(x, mixed, w_out, g, w_gate, w_up, w_down):
    vmem = (4 * FFN_TM * D_MODEL * 4 + 2 * FFN_TM * D_MODEL * 2
            + 2 * D_MODEL * D_MODEL * 2 + 6 * D_MODEL * FFN_TF * 2
            + FFN_TM * D_MODEL * 2 + 4 * FFN_TM * FFN_TF * 4 + 4 * MIB)
    return pl.pallas_call(
        _ffn_kernel,
        out_shape=jax.ShapeDtypeStruct((SEQ, D_MODEL), F32),
        grid=(SEQ // FFN_TM, D_FF // FFN_TF),
        in_specs=[
            pl.BlockSpec((FFN_TM, D_MODEL), lambda i, f: (i, 0)),
            pl.BlockSpec((FFN_TM, D_MODEL), lambda i, f: (i, 0)),
            pl.BlockSpec((D_MODEL, D_MODEL), lambda i, f: (0, 0)),
            pl.BlockSpec((1, D_MODEL), lambda i, f: (0, 0)),
            pl.BlockSpec((D_MODEL, FFN_TF), lambda i, f: (0, f)),
            pl.BlockSpec((D_MODEL, FFN_TF), lambda i, f: (0, f)),
            pl.BlockSpec((FFN_TF, D_MODEL), lambda i, f: (f, 0)),
        ],
        out_specs=pl.BlockSpec((FFN_TM, D_MODEL), lambda i, f: (i, 0)),
        scratch_shapes=[pltpu.VMEM((FFN_TM, D_MODEL), BF16)],
        compiler_params=pltpu.CompilerParams(
            dimension_semantics=("arbitrary", "arbitrary"),
            vmem_limit_bytes=vmem),
        name="ffn",
    )(x, mixed, w_out, g, w_gate, w_up, w_down)


def kernel(x, rel_bias, norm1_g, w_in, sgu_norm_g, sgu_w, sgu_b, q_norm_g, k_norm_g,
           sinks, out_norm_a, out_norm_b, w_out, norm2_g, w_gate, w_up, w_down):
    assert x.shape == (1, SEQ, D_MODEL)
    h = x.reshape(SEQ, D_MODEL)
    w_in_b, w_out_b = w_in.astype(BF16), w_out.astype(BF16)
    w_gate_b, w_up_b, w_down_b = w_gate.astype(BF16), w_up.astype(BF16), w_down.astype(BF16)
    for l in range(DEPTH):
        z = _in_proj(h, norm1_g[l].reshape(1, D_MODEL), w_in_b[l])
        mixed = _mixer(
            z, sinks[l], rel_bias,
            sgu_norm_g[l].reshape(1, SGU_WIDTH), sgu_w[l],
            jnp.repeat(sgu_b[l].T, HEAD_DIM, axis=1),
            jnp.tile(q_norm_g[l], N_Q_HEADS).reshape(1, ATTN_WIDTH),
            jnp.tile(k_norm_g[l], N_KV_HEADS).reshape(1, KV_WIDTH),
            out_norm_a[l].reshape(1, SGU_WIDTH), out_norm_b[l].reshape(1, ATTN_WIDTH))
        h = _ffn(h, mixed, w_out_b[l], norm2_g[l].reshape(1, D_MODEL),
                 w_gate_b[l], w_up_b[l], w_down_b[l])
    return h.reshape(1, SEQ, D_MODEL)
```

```python
import functools
import math

import numpy as np
import jax
import jax.numpy as jnp
from jax import lax
from jax.experimental import pallas as pl
from jax.experimental.pallas import tpu as pltpu

D_MODEL = 2048
SEQ = 8192
DEPTH = 4
HEAD_DIM = 64
SGU_WIDTH = D_MODEL // 2
N_SGU_HEADS = SGU_WIDTH // HEAD_DIM
CHUNK = 128
ATTN_WIDTH = D_MODEL - SGU_WIDTH
N_Q_HEADS = ATTN_WIDTH // HEAD_DIM
N_KV_HEADS = 4
GROUP = N_Q_HEADS // N_KV_HEADS
KV_WIDTH = N_KV_HEADS * HEAD_DIM
WINDOW = 128
BLOCK = 128
NUM_BUCKETS = 32
MAX_DISTANCE = 128
IN_WIDTH = 2 * SGU_WIDTH + ATTN_WIDTH + 2 * KV_WIDTH
D_FF = -(-8 * D_MODEL // (3 * 256)) * 256
EPS = 1e-6
NEG_INF = -1e30
SCALE = 1.0 / math.sqrt(HEAD_DIM)

LANES = 128
HEADS_PER_LANE_BLOCK = LANES // HEAD_DIM
MIB = 1024 * 1024

F32 = jnp.float32
BF16 = jnp.bfloat16

IN_TM = 1024
IN_TN = 512
N_GELU_TILES = (2 * SGU_WIDTH) // IN_TN
FFN_TM = 512
FFN_TF = 512


def _rms_rows(xf, g):
    ms = jnp.mean(xf * xf, axis=-1, keepdims=True)
    return xf * lax.rsqrt(ms + EPS) * g


def _split3(x):
    t1 = x.astype(BF16)
    r = x - t1.astype(F32)
    t2 = r.astype(BF16)
    r = r - t2.astype(F32)
    return t1, t2, r.astype(BF16)


def _dot(a, b):
    return jnp.dot(a, b, preferred_element_type=F32)


def _head_rsqrt(x, seg_ref, exp_ref):
    seg = seg_ref[...]
    a1, a2, a3 = _split3(x * x)
    ss = _dot(a1, seg) + _dot(a2, seg) + _dot(a3, seg)
    r = lax.rsqrt(ss * (1.0 / HEAD_DIM) + EPS)
    r1, r2, r3 = _split3(r)
    ex = exp_ref[...]
    return (_dot(r1, ex) + _dot(r2, ex)) + _dot(r3, ex)


def _in_proj_kernel(x_ref, g_ref, w_ref, o_ref, h_ref):
    j = pl.program_id(1)

    @pl.when(j == 0)
    def _():
        h_ref[...] = _rms_rows(x_ref[...], g_ref[...]).astype(BF16)

    @pl.when(j < N_GELU_TILES)
    def _():
        z = _dot(h_ref[...], w_ref[...])
        o_ref[...] = (0.5 * z * (1.0 + lax.erf(z * (1.0 / math.sqrt(2.0))))).astype(BF16)

    @pl.when(j >= N_GELU_TILES)
    def _():
        o_ref[...] = _dot(h_ref[...], w_ref[...]).astype(BF16)


def _in_proj(x, g, w, layer):
    vmem = (2 * IN_TM * D_MODEL * 4 + 2 * D_MODEL * IN_TN * 2
            + 2 * IN_TM * IN_TN * 2 + 2 * IN_TM * IN_TN * 4 + IN_TM * D_MODEL * 2 + 4 * MIB)
    return pl.pallas_call(
        _in_proj_kernel,
        out_shape=jax.ShapeDtypeStruct((SEQ, IN_WIDTH), BF16),
        grid=(SEQ // IN_TM, IN_WIDTH // IN_TN),
        in_specs=[
            pl.BlockSpec((IN_TM, D_MODEL), lambda i, j: (i, 0)),
            pl.BlockSpec((1, D_MODEL), lambda i, j: (0, 0)),
            pl.BlockSpec((None, D_MODEL, IN_TN), lambda i, j: (layer, 0, j)),
        ],
        out_specs=pl.BlockSpec((IN_TM, IN_TN), lambda i, j: (i, j)),
        scratch_shapes=[pltpu.VMEM((IN_TM, D_MODEL), BF16)],
        compiler_params=pltpu.CompilerParams(
            dimension_semantics=("arbitrary", "arbitrary"),
            vmem_limit_bytes=vmem),
        name="in_proj",
    )(x, g, w)


def _mixer_kernel(sinks_ref, relb_ref,
                  zu_ref, zv_ref, zq_ref, zkv_ref,
                  sgu_g_ref, sgu_w_ref, sgu_b_ref, qg_ref, kg_ref,
                  bucket_ref, seg_ref, exp_ref, ga_ref, gb_ref,
                  o_ref,
                  wm_ref, bias_ref, kband_ref, vband_ref):
    i = pl.program_id(0)
    lane = lax.broadcasted_iota(jnp.int32, (BLOCK, LANES), 1)
    lo_half = lane < HEAD_DIM

    @pl.when(i == 0)
    def _():
        t_idx = lax.broadcasted_iota(jnp.int32, (CHUNK, CHUNK), 0)
        s_idx = lax.broadcasted_iota(jnp.int32, (CHUNK, CHUNK), 1)
        keep = s_idx <= t_idx

        def mask_head(h, c):
            wm_ref[h] = jnp.where(keep, sgu_w_ref[h], 0.0).astype(BF16)
            return c
        lax.fori_loop(0, N_SGU_HEADS, mask_head, 0)

        bucket = bucket_ref[...]

        def bias_head(h, c):
            acc = jnp.zeros((BLOCK, 2 * BLOCK), F32)
            for b in range(NUM_BUCKETS):
                acc = jnp.where(bucket == b, relb_ref[b, h], acc)
            bias_ref[h] = acc
            return c
        lax.fori_loop(0, N_Q_HEADS, bias_head, 0)

        kband_ref[...] = jnp.zeros(kband_ref.shape, BF16)
        vband_ref[...] = jnp.zeros(vband_ref.shape, BF16)

    v = zv_ref[...].astype(F32)
    vn = (v * _head_rsqrt(v, seg_ref, exp_ref) * sgu_g_ref[...]).astype(BF16)
    gates = []
    for p in range(N_SGU_HEADS // HEADS_PER_LANE_BLOCK):
        vb = vn[:, p * LANES:(p + 1) * LANES]
        g_lo = _dot(wm_ref[2 * p], vb)
        g_hi = _dot(wm_ref[2 * p + 1], vb)
        gates.append(jnp.where(lo_half, g_lo, g_hi))
    gate = jnp.concatenate(gates, axis=-1) + sgu_b_ref[...]
    out_a = zu_ref[...].astype(F32) * gate
    o_ref[:, :SGU_WIDTH] = _rms_rows(out_a, ga_ref[...]).astype(BF16)

    q = zq_ref[...].astype(F32)
    qn = (q * _head_rsqrt(q, seg_ref, exp_ref) * qg_ref[...] * SCALE).astype(BF16)
    kv = zkv_ref[...].astype(F32)
    k = kv[:, :KV_WIDTH]
    kn = k * _head_rsqrt(k, seg_ref.at[:KV_WIDTH], exp_ref.at[:, :KV_WIDTH]) * kg_ref[...]
    vv = kv[:, KV_WIDTH:]

    for j in range(N_KV_HEADS):
        p, half = divmod(j, HEADS_PER_LANE_BLOCK)
        kb = kn[:, p * LANES:(p + 1) * LANES]
        vb = vv[:, p * LANES:(p + 1) * LANES]
        keep = lo_half if half == 0 else jnp.logical_not(lo_half)
        k_same = jnp.where(keep, kb, 0.0)
        v_same = jnp.where(keep, vb, 0.0)
        k_other = pltpu.roll(k_same, HEAD_DIM, 1)
        v_other = pltpu.roll(v_same, HEAD_DIM, 1)
        kband_ref[j, half, BLOCK:, :] = k_same.astype(BF16)
        kband_ref[j, 1 - half, BLOCK:, :] = k_other.astype(BF16)
        vband_ref[j, half, BLOCK:, :] = v_same.astype(BF16)
        vband_ref[j, 1 - half, BLOCK:, :] = v_other.astype(BF16)

    qi = lax.broadcasted_iota(jnp.int32, (BLOCK, 2 * BLOCK), 0)
    kj = lax.broadcasted_iota(jnp.int32, (BLOCK, 2 * BLOCK), 1)
    dist = qi + BLOCK - kj
    first_key = jnp.where(i > 0, 0, BLOCK)
    valid = (dist >= 0) & (dist < WINDOW) & (kj >= first_key)

    outs = []
    for p in range(N_Q_HEADS // HEADS_PER_LANE_BLOCK):
        qb = qn[:, p * LANES:(p + 1) * LANES]
        j = (p * HEADS_PER_LANE_BLOCK) // GROUP
        acc = None
        for half in range(HEADS_PER_LANE_BLOCK):
            h = p * HEADS_PER_LANE_BLOCK + half
            s = lax.dot_general(qb, kband_ref[j, half], (((1,), (1,)), ((), ())),
                                preferred_element_type=F32)
            s = jnp.where(valid, s + bias_ref[h], NEG_INF)
            sink = sinks_ref[h]
            m = jnp.maximum(jnp.max(s, axis=-1, keepdims=True), sink)
            e = jnp.exp(s - m)
            den = jnp.sum(e, axis=-1, keepdims=True) + jnp.exp(sink - m)
            prob = (e * (1.0 / den)).astype(BF16)
            o = _dot(prob, vband_ref[j, half])
            acc = o if acc is None else acc + o
        outs.append(acc)
    out_b = jnp.concatenate(outs, axis=-1)
    o_ref[:, SGU_WIDTH:] = _rms_rows(out_b, gb_ref[...]).astype(BF16)

    kband_ref[:, :, :BLOCK, :] = kband_ref[:, :, BLOCK:, :]
    vband_ref[:, :, :BLOCK, :] = vband_ref[:, :, BLOCK:, :]


def _t5_bucket_table():
    qi = np.arange(BLOCK)[:, None]
    kj = np.arange(2 * BLOCK)[None, :]
    n = np.maximum(qi + BLOCK - kj, 0)
    max_exact = NUM_BUCKETS // 2
    nf = np.maximum(n, 1).astype(np.float32)
    large = max_exact + (np.log(nf / np.float32(max_exact))
                         / np.float32(math.log(MAX_DISTANCE / max_exact))
                         * np.float32(NUM_BUCKETS - max_exact)).astype(np.int32)
    large = np.minimum(large, NUM_BUCKETS - 1)
    return np.where(n < max_exact, n, large).astype(np.int32)


def _head_segment_matrices():
    lane = np.arange(SGU_WIDTH)
    seg = (lane[:, None] // HEAD_DIM == np.arange(LANES)[None, :])
    return seg.astype(np.float32), seg.T.astype(np.float32)


def _mixer(z, sinks, rel_bias, sgu_g, sgu_w, sgu_b, qg, kg, ga, gb):
    seg, exp = _head_segment_matrices()
    row = lambda n: pl.BlockSpec((1, n), lambda i: (0, 0))
    smem = pl.BlockSpec(memory_space=pltpu.SMEM)
    zcol = lambda w, c: pl.BlockSpec((BLOCK, w), lambda i: (i, c))
    return pl.pallas_call(
        _mixer_kernel,
        out_shape=jax.ShapeDtypeStruct((SEQ, D_MODEL), BF16),
        grid=(SEQ // BLOCK,),
        in_specs=[
            smem, smem,
            zcol(SGU_WIDTH, 0), zcol(SGU_WIDTH, 1), zcol(ATTN_WIDTH, 2),
            zcol(2 * KV_WIDTH, (2 * SGU_WIDTH + ATTN_WIDTH) // (2 * KV_WIDTH)),
            row(SGU_WIDTH),
            pl.BlockSpec((N_SGU_HEADS, CHUNK, CHUNK), lambda i: (0, 0, 0)),
            pl.BlockSpec((CHUNK, SGU_WIDTH), lambda i: (0, 0)),
            row(ATTN_WIDTH), row(KV_WIDTH),
            pl.BlockSpec((BLOCK, 2 * BLOCK), lambda i: (0, 0)),
            pl.BlockSpec((SGU_WIDTH, LANES), lambda i: (0, 0)),
            pl.BlockSpec((LANES, SGU_WIDTH), lambda i: (0, 0)),
            row(SGU_WIDTH), row(ATTN_WIDTH),
        ],
        out_specs=pl.BlockSpec((BLOCK, D_MODEL), lambda i: (i, 0)),
        scratch_shapes=[
            pltpu.VMEM((N_SGU_HEADS, CHUNK, CHUNK), BF16),
            pltpu.VMEM((N_Q_HEADS, BLOCK, 2 * BLOCK), F32),
            pltpu.VMEM((N_KV_HEADS, HEADS_PER_LANE_BLOCK, 2 * BLOCK, LANES), BF16),
            pltpu.VMEM((N_KV_HEADS, HEADS_PER_LANE_BLOCK, 2 * BLOCK, LANES), BF16),
        ],
        compiler_params=pltpu.CompilerParams(
            dimension_semantics=("arbitrary",),
            vmem_limit_bytes=32 * MIB),
        name="mixer",
    )(sinks, rel_bias, z, z, z, z, sgu_g, sgu_w, sgu_b, qg, kg,
      jnp.asarray(_t5_bucket_table()), jnp.asarray(seg, BF16), jnp.asarray(exp, BF16),
      ga, gb)


def _ffn_kernel(x_ref, mix_ref, wo_ref, g_ref, wg_ref, wu_ref, wd_ref, o_ref, h_ref):
    f = pl.program_id(1)

    @pl.when(f == 0)
    def _():
        x1 = x_ref[...] + _dot(mix_ref[...], wo_ref[...])
        o_ref[...] = x1
        h_ref[...] = _rms_rows(x1, g_ref[...]).astype(BF16)

    h = h_ref[...]
    gate = _dot(h, wg_ref[...])
    up = _dot(h, wu_ref[...])
    act = (gate * jax.nn.sigmoid(gate) * up).astype(BF16)
    o_ref[...] += _dot(act, wd_ref[...])


def _ffn(x, mixed, w_out, g, w_gate, w_up, w_down, layer):
    vmem = (4 * FFN_TM * D_MODEL * 4 + 2 * FFN_TM * D_MODEL * 2
            + 2 * D_MODEL * D_MODEL * 2 + 6 * D_MODEL * FFN_TF * 2
            + FFN_TM * D_MODEL * 2 + 4 * FFN_TM * FFN_TF * 4 + 4 * MIB)
    return pl.pallas_call(
        _ffn_kernel,
        out_shape=jax.ShapeDtypeStruct((SEQ, D_MODEL), F32),
        grid=(SEQ // FFN_TM, D_FF // FFN_TF),
        in_specs=[
            pl.BlockSpec((FFN_TM, D_MODEL), lambda i, f: (i, 0)),
            pl.BlockSpec((FFN_TM, D_MODEL), lambda i, f: (i, 0)),
            pl.BlockSpec((None, D_MODEL, D_MODEL), lambda i, f: (layer, 0, 0)),
            pl.BlockSpec((1, D_MODEL), lambda i, f: (0, 0)),
            pl.BlockSpec((None, D_MODEL, FFN_TF), lambda i, f: (layer, 0, f)),
            pl.BlockSpec((None, D_MODEL, FFN_TF), lambda i, f: (layer, 0, f)),
            pl.BlockSpec((None, FFN_TF, D_MODEL), lambda i, f: (layer, f, 0)),
        ],
        out_specs=pl.BlockSpec((FFN_TM, D_MODEL), lambda i, f: (i, 0)),
        scratch_shapes=[pltpu.VMEM((FFN_TM, D_MODEL), BF16)],
        compiler_params=pltpu.CompilerParams(
            dimension_semantics=("arbitrary", "arbitrary"),
            vmem_limit_bytes=vmem),
        name="ffn",
    )(x, mixed, w_out, g, w_gate, w_up, w_down)


def kernel(x, rel_bias, norm1_g, w_in, sgu_norm_g, sgu_w, sgu_b, q_norm_g, k_norm_g,
           sinks, out_norm_a, out_norm_b, w_out, norm2_g, w_gate, w_up, w_down):
    assert x.shape == (1, SEQ, D_MODEL)
    h = x.reshape(SEQ, D_MODEL)
    w_in_b, w_out_b = w_in.astype(BF16), w_out.astype(BF16)
    w_gate_b, w_up_b, w_down_b = w_gate.astype(BF16), w_up.astype(BF16), w_down.astype(BF16)
    for l in range(DEPTH):
        z = _in_proj(h, norm1_g[l].reshape(1, D_MODEL), w_in_b, l)
        mixed = _mixer(
            z, sinks[l], rel_bias,
            sgu_norm_g[l].reshape(1, SGU_WIDTH), sgu_w[l],
            jnp.repeat(sgu_b[l].T, HEAD_DIM, axis=1),
            jnp.tile(q_norm_g[l], N_Q_HEADS).reshape(1, ATTN_WIDTH),
            jnp.tile(k_norm_g[l], N_KV_HEADS).reshape(1, KV_WIDTH),
            out_norm_a[l].reshape(1, SGU_WIDTH), out_norm_b[l].reshape(1, ATTN_WIDTH))
        h = _ffn(h, mixed, w_out_b, norm2_g[l].reshape(1, D_MODEL),
                 w_gate_b, w_up_b, w_down_b, l)
    return h.reshape(1, SEQ, D_MODEL)
```

```python
import functools
import math

import numpy as np
import jax
import jax.numpy as jnp
from jax import lax
from jax.experimental import pallas as pl
from jax.experimental.pallas import tpu as pltpu

D_MODEL = 2048
SEQ = 8192
DEPTH = 4
HEAD_DIM = 64
SGU_WIDTH = D_MODEL // 2
N_SGU_HEADS = SGU_WIDTH // HEAD_DIM
CHUNK = 128
ATTN_WIDTH = D_MODEL - SGU_WIDTH
N_Q_HEADS = ATTN_WIDTH // HEAD_DIM
N_KV_HEADS = 4
GROUP = N_Q_HEADS // N_KV_HEADS
KV_WIDTH = N_KV_HEADS * HEAD_DIM
WINDOW = 128
BLOCK = 128
BAND = 2 * BLOCK
NUM_BUCKETS = 32
MAX_DISTANCE = 128
IN_WIDTH = 2 * SGU_WIDTH + ATTN_WIDTH + 2 * KV_WIDTH
D_FF = -(-8 * D_MODEL // (3 * 256)) * 256
EPS = 1e-6
NEG_INF = -1e30
SCALE = 1.0 / math.sqrt(HEAD_DIM)

LANES = 128
HEADS_PER_LANE_BLOCK = LANES // HEAD_DIM
N_LANE_BLOCKS = SGU_WIDTH // LANES
MIB = 1024 * 1024

F32 = jnp.float32
BF16 = jnp.bfloat16

IN_TM = 1024
IN_TN = 512
N_GELU_TILES = (2 * SGU_WIDTH) // IN_TN
FFN_TM = 512
FFN_TF = 512

assert CHUNK == BLOCK == LANES and HEADS_PER_LANE_BLOCK == 2 and GROUP == 4


def _rms_rows(xf, g):
    ms = jnp.mean(xf * xf, axis=-1, keepdims=True)
    return xf * lax.rsqrt(ms + EPS) * g


def _split2(x):
    hi = x.astype(BF16)
    return hi, (x - hi.astype(F32)).astype(BF16)


def _dot(a, b):
    return jnp.dot(a, b, preferred_element_type=F32)


def _dot_nt(a, b):
    return lax.dot_general(a, b, (((1,), (1,)), ((), ())), preferred_element_type=F32)


def _in_proj_kernel(x_ref, g_ref, w_ref, o_ref, h_ref):
    j = pl.program_id(1)

    @pl.when(j == 0)
    def _():
        h_ref[...] = _rms_rows(x_ref[...], g_ref[...]).astype(BF16)

    @pl.when(j < N_GELU_TILES)
    def _():
        z = _dot(h_ref[...], w_ref[...])
        o_ref[...] = (0.5 * z * (1.0 + lax.erf(z * (1.0 / math.sqrt(2.0))))).astype(BF16)

    @pl.when(j >= N_GELU_TILES)
    def _():
        o_ref[...] = _dot(h_ref[...], w_ref[...]).astype(BF16)


def _in_proj(x, g, w, layer):
    vmem = (2 * IN_TM * D_MODEL * 4 + 2 * D_MODEL * IN_TN * 2
            + 2 * IN_TM * IN_TN * 2 + 2 * IN_TM * IN_TN * 4 + IN_TM * D_MODEL * 2 + 4 * MIB)
    return pl.pallas_call(
        _in_proj_kernel,
        out_shape=jax.ShapeDtypeStruct((SEQ, IN_WIDTH), BF16),
        grid=(SEQ // IN_TM, IN_WIDTH // IN_TN),
        in_specs=[
            pl.BlockSpec((IN_TM, D_MODEL), lambda i, j: (i, 0)),
            pl.BlockSpec((1, D_MODEL), lambda i, j: (0, 0)),
            pl.BlockSpec((None, D_MODEL, IN_TN), lambda i, j: (layer, 0, j)),
        ],
        out_specs=pl.BlockSpec((IN_TM, IN_TN), lambda i, j: (i, j)),
        scratch_shapes=[pltpu.VMEM((IN_TM, D_MODEL), BF16)],
        compiler_params=pltpu.CompilerParams(
            dimension_semantics=("arbitrary", "arbitrary"),
            vmem_limit_bytes=vmem),
        name="in_proj",
    )(x, g, w)


def _mixer_kernel(sinks_ref, relb_ref,
                  zu_ref, zv_ref, zq_ref, zkv_ref, zkvp_ref,
                  sgu_g_ref, sgu_w_ref, sgu_b_ref, qg_ref, kg_ref,
                  bucket_ref, seg_ref, exp_ref, ga_ref, gb_ref,
                  o_ref,
                  wm_ref, bias_ref, kband_ref, vband_ref, sq_ref, rr_ref):
    i = pl.program_id(0)
    lane = lax.broadcasted_iota(jnp.int32, (BLOCK, LANES), 1)
    lo_half = lane < HEAD_DIM

    @pl.when(i == 0)
    def _():
        t_idx = lax.broadcasted_iota(jnp.int32, (CHUNK, CHUNK), 0)
        s_idx = lax.broadcasted_iota(jnp.int32, (CHUNK, CHUNK), 1)
        keep = s_idx <= t_idx

        def mask_pair(p, c):
            wm_ref[p, :CHUNK, :] = jnp.where(keep, sgu_w_ref[2 * p], 0.0).astype(BF16)
            wm_ref[p, CHUNK:, :] = jnp.where(keep, sgu_w_ref[2 * p + 1], 0.0).astype(BF16)
            return c
        lax.fori_loop(0, N_LANE_BLOCKS, mask_pair, 0)

        bucket = bucket_ref[...]
        for rb in range(2):
            for slot in range(2):
                def bias_head(j, c, rb=rb, slot=slot):
                    h = GROUP * j + HEADS_PER_LANE_BLOCK * rb + slot
                    acc = jnp.zeros((BLOCK, BAND), F32)
                    for b in range(NUM_BUCKETS):
                        acc = jnp.where(bucket == b, relb_ref[b, h], acc)
                    bias_ref[j, rb * BLOCK:(rb + 1) * BLOCK, slot * BAND:(slot + 1) * BAND] = acc
                    return c
                lax.fori_loop(0, N_KV_HEADS, bias_head, 0)

        row = lax.broadcasted_iota(jnp.int32, (2 * BAND, 2 * LANES), 0)
        col = lax.broadcasted_iota(jnp.int32, (2 * BAND, 2 * LANES), 1)
        ones = (((row < BAND) & (col >= LANES) & (col < LANES + HEAD_DIM))
                | ((row >= BAND) & (col >= LANES + HEAD_DIM)))
        pattern = jnp.where(ones, 1.0, 0.0).astype(BF16)
        for j in range(N_KV_HEADS):
            vband_ref[j] = pattern

    v = zv_ref[...].astype(F32)
    q = zq_ref[...].astype(F32)
    kv = jnp.concatenate([zkvp_ref[...], zkv_ref[...]], axis=0).astype(F32)
    k = kv[:, :KV_WIDTH]
    vv = kv[:, KV_WIDTH:]

    v2h, v2l = _split2(v * v)
    q2h, q2l = _split2(q * q)
    sq_ref[0 * BLOCK:1 * BLOCK] = v2h
    sq_ref[1 * BLOCK:2 * BLOCK] = q2h
    sq_ref[2 * BLOCK:3 * BLOCK] = v2l
    sq_ref[3 * BLOCK:4 * BLOCK] = q2l
    ss = _dot(sq_ref[...], seg_ref[...])
    r = lax.rsqrt((ss[:2 * BLOCK] + ss[2 * BLOCK:]) * (1.0 / HEAD_DIM) + EPS)
    rh, rl = _split2(r)
    rr_ref[:2 * BLOCK] = rh
    rr_ref[2 * BLOCK:] = rl
    rb = _dot(rr_ref[...], exp_ref[...])
    rb = rb[:2 * BLOCK] + rb[2 * BLOCK:]
    vn = (v * rb[:BLOCK] * sgu_g_ref[...]).astype(BF16)
    qn = (q * rb[BLOCK:] * (qg_ref[...] * SCALE)).astype(BF16)

    segk = seg_ref[:KV_WIDTH, :]
    expk = exp_ref[:, :KV_WIDTH]
    k2h, k2l = _split2(k * k)
    rk = lax.rsqrt((_dot(k2h, segk) + _dot(k2l, segk)) * (1.0 / HEAD_DIM) + EPS)
    rkh, rkl = _split2(rk)
    kn = k * (_dot(rkh, expk) + _dot(rkl, expk)) * kg_ref[...]

    gates = []
    for p in range(N_LANE_BLOCKS):
        g2 = _dot(wm_ref[p], vn[:, p * LANES:(p + 1) * LANES])
        gates.append(jnp.where(lo_half, g2[:CHUNK], g2[CHUNK:]))
    gate = jnp.concatenate(gates, axis=-1) + sgu_b_ref[...]
    out_a = zu_ref[...].astype(F32) * gate
    o_ref[:, :SGU_WIDTH] = _rms_rows(out_a, ga_ref[...]).astype(BF16)

    lo_band = lax.broadcasted_iota(jnp.int32, (BAND, LANES), 1) < HEAD_DIM
    for j in range(N_KV_HEADS):
        p, half = divmod(j, HEADS_PER_LANE_BLOCK)
        keep = lo_band if half == 0 else jnp.logical_not(lo_band)
        k_same = jnp.where(keep, kn[:, p * LANES:(p + 1) * LANES], 0.0)
        v_same = jnp.where(keep, vv[:, p * LANES:(p + 1) * LANES], 0.0)
        k_other = pltpu.roll(k_same, HEAD_DIM, 1)
        v_other = pltpu.roll(v_same, HEAD_DIM, 1)
        k_slots = (k_same, k_other) if half == 0 else (k_other, k_same)
        v_slots = (v_same, v_other) if half == 0 else (v_other, v_same)
        for slot in range(2):
            rows = slice(slot * BAND, (slot + 1) * BAND)
            kband_ref[j, rows, :] = k_slots[slot].astype(BF16)
            vband_ref[j, rows, :LANES] = v_slots[slot].astype(BF16)

    qi = lax.broadcasted_iota(jnp.int32, (2 * BLOCK, 2 * BAND), 0) & (BLOCK - 1)
    kj = lax.broadcasted_iota(jnp.int32, (2 * BLOCK, 2 * BAND), 1) & (BAND - 1)
    dist = qi + BLOCK - kj
    first_key = jnp.where(i > 0, 0, BLOCK)
    valid = (dist >= 0) & (dist < WINDOW) & (kj >= first_key)
    top_rows = lax.broadcasted_iota(jnp.int32, (2 * BLOCK, 1), 0) < BLOCK
    lo_half2 = lax.broadcasted_iota(jnp.int32, (2 * BLOCK, LANES), 1) < HEAD_DIM

    out_blocks = []
    for j in range(N_KV_HEADS):
        q2 = jnp.concatenate([qn[:, (2 * j) * LANES:(2 * j + 1) * LANES],
                              qn[:, (2 * j + 1) * LANES:(2 * j + 2) * LANES]], axis=0)
        s = _dot_nt(q2, kband_ref[j])
        s = jnp.where(valid, s + bias_ref[j], NEG_INF)
        es, sink_terms = [], []
        for slot in range(2):
            sc = s[:, slot * BAND:(slot + 1) * BAND]
            sink = jnp.where(top_rows, sinks_ref[GROUP * j + slot],
                             sinks_ref[GROUP * j + HEADS_PER_LANE_BLOCK + slot])
            m = jnp.maximum(jnp.max(sc, axis=-1, keepdims=True), sink)
            es.append(jnp.exp(sc - m).astype(BF16))
            sink_terms.append(jnp.exp(sink - m))
        o = _dot(jnp.concatenate(es, axis=-1), vband_ref[j])
        den = o[:, LANES:] + jnp.where(lo_half2, sink_terms[0], sink_terms[1])
        outp = o[:, :LANES] * (1.0 / den)
        out_blocks += [outp[:BLOCK], outp[BLOCK:]]
    out_b = jnp.concatenate(out_blocks, axis=-1)
    o_ref[:, SGU_WIDTH:] = _rms_rows(out_b, gb_ref[...]).astype(BF16)


def _t5_bucket_table():
    qi = np.arange(BLOCK)[:, None]
    kj = np.arange(BAND)[None, :]
    n = np.maximum(qi + BLOCK - kj, 0)
    max_exact = NUM_BUCKETS // 2
    nf = np.maximum(n, 1).astype(np.float32)
    large = max_exact + (np.log(nf / np.float32(max_exact))
                         / np.float32(math.log(MAX_DISTANCE / max_exact))
                         * np.float32(NUM_BUCKETS - max_exact)).astype(np.int32)
    large = np.minimum(large, NUM_BUCKETS - 1)
    return np.where(n < max_exact, n, large).astype(np.int32)


def _head_segment_matrices():
    lane = np.arange(SGU_WIDTH)
    seg = (lane[:, None] // HEAD_DIM == np.arange(LANES)[None, :])
    return seg.astype(np.float32), seg.T.astype(np.float32)


def _mixer(z, sinks, rel_bias, sgu_g, sgu_w, sgu_b, qg, kg, ga, gb):
    seg, exp = _head_segment_matrices()
    row = lambda n: pl.BlockSpec((1, n), lambda i: (0, 0))
    smem = pl.BlockSpec(memory_space=pltpu.SMEM)
    zcol = lambda w, c: pl.BlockSpec((BLOCK, w), lambda i: (i, c))
    kv_col = (2 * SGU_WIDTH + ATTN_WIDTH) // (2 * KV_WIDTH)
    return pl.pallas_call(
        _mixer_kernel,
        out_shape=jax.ShapeDtypeStruct((SEQ, D_MODEL), BF16),
        grid=(SEQ // BLOCK,),
        in_specs=[
            smem, smem,
            zcol(SGU_WIDTH, 0), zcol(SGU_WIDTH, 1), zcol(ATTN_WIDTH, 2),
            zcol(2 * KV_WIDTH, kv_col),
            pl.BlockSpec((BLOCK, 2 * KV_WIDTH), lambda i: (jnp.maximum(i - 1, 0), kv_col)),
            row(SGU_WIDTH),
            pl.BlockSpec((N_SGU_HEADS, CHUNK, CHUNK), lambda i: (0, 0, 0)),
            pl.BlockSpec((CHUNK, SGU_WIDTH), lambda i: (0, 0)),
            row(ATTN_WIDTH), row(KV_WIDTH),
            pl.BlockSpec((BLOCK, BAND), lambda i: (0, 0)),
            pl.BlockSpec((SGU_WIDTH, LANES), lambda i: (0, 0)),
            pl.BlockSpec((LANES, SGU_WIDTH), lambda i: (0, 0)),
            row(SGU_WIDTH), row(ATTN_WIDTH),
        ],
        out_specs=pl.BlockSpec((BLOCK, D_MODEL), lambda i: (i, 0)),
        scratch_shapes=[
            pltpu.VMEM((N_LANE_BLOCKS, 2 * CHUNK, CHUNK), BF16),
            pltpu.VMEM((N_KV_HEADS, 2 * BLOCK, 2 * BAND), F32),
            pltpu.VMEM((N_KV_HEADS, 2 * BAND, LANES), BF16),
            pltpu.VMEM((N_KV_HEADS, 2 * BAND, 2 * LANES), BF16),
            pltpu.VMEM((4 * BLOCK, SGU_WIDTH), BF16),
            pltpu.VMEM((4 * BLOCK, LANES), BF16),
        ],
        compiler_params=pltpu.CompilerParams(
            dimension_semantics=("arbitrary",),
            vmem_limit_bytes=32 * MIB),
        name="mixer",
    )(sinks, rel_bias, z, z, z, z, z, sgu_g, sgu_w, sgu_b, qg, kg,
      jnp.asarray(_t5_bucket_table()), jnp.asarray(seg, BF16), jnp.asarray(exp, BF16),
      ga, gb)


def _ffn_kernel(x_ref, mix_ref, wo_ref, g_ref, wg_ref, wu_ref, wd_ref, o_ref, h_ref):
    f = pl.program_id(1)

    @pl.when(f == 0)
    def _():
        x1 = x_ref[...] + _dot(mix_ref[...], wo_ref[...])
        o_ref[...] = x1
        h_ref[...] = _rms_rows(x1, g_ref[...]).astype(BF16)

    h = h_ref[...]
    gate = _dot(h, wg_ref[...])
    up = _dot(h, wu_ref[...])
    act = (gate * jax.nn.sigmoid(gate) * up).astype(BF16)
    o_ref[...] += _dot(act, wd_ref[...])


def _ffn(x, mixed, w_out, g, w_gate, w_up, w_down, layer):
    vmem = (4 * FFN_TM * D_MODEL * 4 + 2 * FFN_TM * D_MODEL * 2
            + 2 * D_MODEL * D_MODEL * 2 + 6 * D_MODEL * FFN_TF * 2
            + FFN_TM * D_MODEL * 2 + 4 * FFN_TM * FFN_TF * 4 + 4 * MIB)
    return pl.pallas_call(
        _ffn_kernel,
        out_shape=jax.ShapeDtypeStruct((SEQ, D_MODEL), F32),
        grid=(SEQ // FFN_TM, D_FF // FFN_TF),
        in_specs=[
            pl.BlockSpec((FFN_TM, D_MODEL), lambda i, f: (i, 0)),
            pl.BlockSpec((FFN_TM, D_MODEL), lambda i, f: (i, 0)),
            pl.BlockSpec((None, D_MODEL, D_MODEL), lambda i, f: (layer, 0, 0)),
            pl.BlockSpec((1, D_MODEL), lambda i, f: (0, 0)),
            pl.BlockSpec((None, D_MODEL, FFN_TF), lambda i, f: (layer, 0, f)),
            pl.BlockSpec((None, D_MODEL, FFN_TF), lambda i, f: (layer, 0, f)),
            pl.BlockSpec((None, FFN_TF, D_MODEL), lambda i, f: (layer, f, 0)),
        ],
        out_specs=pl.BlockSpec((FFN_TM, D_MODEL), lambda i, f: (i, 0)),
        scratch_shapes=[pltpu.VMEM((FFN_TM, D_MODEL), BF16)],
        compiler_params=pltpu.CompilerParams(
            dimension_semantics=("arbitrary", "arbitrary"),
            vmem_limit_bytes=vmem),
        name="ffn",
    )(x, mixed, w_out, g, w_gate, w_up, w_down)


def kernel(x, rel_bias, norm1_g, w_in, sgu_norm_g, sgu_w, sgu_b, q_norm_g, k_norm_g,
           sinks, out_norm_a, out_norm_b, w_out, norm2_g, w_gate, w_up, w_down):
    assert x.shape == (1, SEQ, D_MODEL)
    h = x.reshape(SEQ, D_MODEL)
    w_in_b, w_out_b = w_in.astype(BF16), w_out.astype(BF16)
    w_gate_b, w_up_b, w_down_b = w_gate.astype(BF16), w_up.astype(BF16), w_down.astype(BF16)
    for l in range(DEPTH):
        z = _in_proj(h, norm1_g[l].reshape(1, D_MODEL), w_in_b, l)
        mixed = _mixer(
            z, sinks[l], rel_bias,
            sgu_norm_g[l].reshape(1, SGU_WIDTH), sgu_w[l],
            jnp.repeat(sgu_b[l].T, HEAD_DIM, axis=1),
            jnp.tile(q_norm_g[l], N_Q_HEADS).reshape(1, ATTN_WIDTH),
            jnp.tile(k_norm_g[l], N_KV_HEADS).reshape(1, KV_WIDTH),
            out_norm_a[l].reshape(1, SGU_WIDTH), out_norm_b[l].reshape(1, ATTN_WIDTH))
        h = _ffn(h, mixed, w_out_b, norm2_g[l].reshape(1, D_MODEL),
                 w_gate_b, w_up_b, w_down_b, l)
    return h.reshape(1, SEQ, D_MODEL)
```

```python
import functools
import math

import numpy as np
import jax
import jax.numpy as jnp
from jax import lax
from jax.experimental import pallas as pl
from jax.experimental.pallas import tpu as pltpu

D_MODEL = 2048
SEQ = 8192
DEPTH = 4
HEAD_DIM = 64
SGU_WIDTH = D_MODEL // 2
N_SGU_HEADS = SGU_WIDTH // HEAD_DIM
CHUNK = 128
ATTN_WIDTH = D_MODEL - SGU_WIDTH
N_Q_HEADS = ATTN_WIDTH // HEAD_DIM
N_KV_HEADS = 4
GROUP = N_Q_HEADS // N_KV_HEADS
KV_WIDTH = N_KV_HEADS * HEAD_DIM
WINDOW = 128
BLOCK = 128
BAND = 2 * BLOCK
NUM_BUCKETS = 32
MAX_DISTANCE = 128
IN_WIDTH = 2 * SGU_WIDTH + ATTN_WIDTH + 2 * KV_WIDTH
D_FF = -(-8 * D_MODEL // (3 * 256)) * 256
EPS = 1e-6
NEG_INF = -1e30
SCALE = 1.0 / math.sqrt(HEAD_DIM)

LANES = 128
BF16_SUBLANES = 16
HEADS_PER_LANE_BLOCK = LANES // HEAD_DIM
N_LANE_BLOCKS = SGU_WIDTH // LANES
MIB = 1024 * 1024

F32 = jnp.float32
BF16 = jnp.bfloat16

IN_TM = 1024
IN_TN = 512
N_GELU_TILES = (2 * SGU_WIDTH) // IN_TN
FFN_TM = 512
FFN_TF = 512
MIXER_VMEM_BYTES = 16 * MIB

assert CHUNK == BLOCK == LANES and HEADS_PER_LANE_BLOCK == 2 and GROUP == 4


def _rms_rows(xf, g):
    ms = jnp.mean(xf * xf, axis=-1, keepdims=True)
    return xf * lax.rsqrt(ms + EPS) * g


def _split2(x):
    hi = x.astype(BF16)
    return hi, (x - hi.astype(F32)).astype(BF16)


def _dot(a, b):
    return jnp.dot(a, b, preferred_element_type=F32)


def _dot_nt(a, b):
    return lax.dot_general(a, b, (((1,), (1,)), ((), ())), preferred_element_type=F32)


def _in_proj_kernel(x_ref, g_ref, w_ref, o_ref, h_ref):
    j = pl.program_id(1)

    @pl.when(j == 0)
    def _():
        h_ref[...] = _rms_rows(x_ref[...], g_ref[...]).astype(BF16)

    @pl.when(j < N_GELU_TILES)
    def _():
        z = _dot(h_ref[...], w_ref[...])
        o_ref[...] = (0.5 * z * (1.0 + lax.erf(z * (1.0 / math.sqrt(2.0))))).astype(BF16)

    @pl.when(j >= N_GELU_TILES)
    def _():
        o_ref[...] = _dot(h_ref[...], w_ref[...]).astype(BF16)


def _in_proj(x, g, w):
    vmem = (2 * IN_TM * D_MODEL * 4 + 2 * D_MODEL * IN_TN * 2
            + 2 * IN_TM * IN_TN * 2 + 2 * IN_TM * IN_TN * 4 + IN_TM * D_MODEL * 2 + 4 * MIB)
    return pl.pallas_call(
        _in_proj_kernel,
        out_shape=jax.ShapeDtypeStruct((SEQ, IN_WIDTH), BF16),
        grid=(SEQ // IN_TM, IN_WIDTH // IN_TN),
        in_specs=[
            pl.BlockSpec((IN_TM, D_MODEL), lambda i, j: (i, 0)),
            pl.BlockSpec((1, D_MODEL), lambda i, j: (0, 0)),
            pl.BlockSpec((D_MODEL, IN_TN), lambda i, j: (0, j)),
        ],
        out_specs=pl.BlockSpec((IN_TM, IN_TN), lambda i, j: (i, j)),
        scratch_shapes=[pltpu.VMEM((IN_TM, D_MODEL), BF16)],
        compiler_params=pltpu.CompilerParams(
            dimension_semantics=("arbitrary", "arbitrary"),
            vmem_limit_bytes=vmem),
        name="in_proj",
    )(x, g, w)


def _mixer_kernel(sinks_ref, relb_ref,
                  zu_ref, zv_ref, zq_ref, zkv_ref, zkvp_ref,
                  sgu_g_ref, sgu_w_ref, sgu_b_ref, qg_ref, kg_ref,
                  bucket_ref, seg_ref, exp_ref, ga_ref, gb_ref,
                  *rest, cast_holds):
    n_cast = len(cast_holds)
    cast_in = rest[:n_cast]
    o_ref = rest[n_cast]
    cast_out = rest[n_cast + 1:2 * n_cast + 1]
    wm_ref, bias_ref, kband_ref, vband_ref, sq_ref, rr_ref = rest[2 * n_cast + 1:]
    i = pl.program_id(0)

    for src, dst, hold in zip(cast_in, cast_out, cast_holds):
        if hold == 1:
            dst[...] = src[...].astype(BF16)
        else:
            @pl.when(i % hold == 0)
            def _(src=src, dst=dst):
                dst[...] = src[...].astype(BF16)

    lane = lax.broadcasted_iota(jnp.int32, (BLOCK, LANES), 1)
    lo_half = lane < HEAD_DIM

    @pl.when(i == 0)
    def _():
        t_idx = lax.broadcasted_iota(jnp.int32, (CHUNK, CHUNK), 0)
        s_idx = lax.broadcasted_iota(jnp.int32, (CHUNK, CHUNK), 1)
        keep = s_idx <= t_idx

        def mask_pair(p, c):
            wm_ref[p, :CHUNK, :] = jnp.where(keep, sgu_w_ref[2 * p], 0.0).astype(BF16)
            wm_ref[p, CHUNK:, :] = jnp.where(keep, sgu_w_ref[2 * p + 1], 0.0).astype(BF16)
            return c
        lax.fori_loop(0, N_LANE_BLOCKS, mask_pair, 0)

        bucket = bucket_ref[...]
        for rb in range(2):
            for slot in range(2):
                def bias_head(j, c, rb=rb, slot=slot):
                    h = GROUP * j + HEADS_PER_LANE_BLOCK * rb + slot
                    acc = jnp.zeros((BLOCK, BAND), F32)
                    for b in range(NUM_BUCKETS):
                        acc = jnp.where(bucket == b, relb_ref[b, h], acc)
                    bias_ref[j, rb * BLOCK:(rb + 1) * BLOCK, slot * BAND:(slot + 1) * BAND] = acc
                    return c
                lax.fori_loop(0, N_KV_HEADS, bias_head, 0)

        row = lax.broadcasted_iota(jnp.int32, (2 * BAND, 2 * LANES), 0)
        col = lax.broadcasted_iota(jnp.int32, (2 * BAND, 2 * LANES), 1)
        ones = (((row < BAND) & (col >= LANES) & (col < LANES + HEAD_DIM))
                | ((row >= BAND) & (col >= LANES + HEAD_DIM)))
        pattern = jnp.where(ones, 1.0, 0.0).astype(BF16)
        for j in range(N_KV_HEADS):
            vband_ref[j] = pattern

    v = zv_ref[...].astype(F32)
    q = zq_ref[...].astype(F32)
    kv = jnp.concatenate([zkvp_ref[...], zkv_ref[...]], axis=0).astype(F32)
    k = kv[:, :KV_WIDTH]
    vv = kv[:, KV_WIDTH:]

    v2h, v2l = _split2(v * v)
    q2h, q2l = _split2(q * q)
    sq_ref[0 * BLOCK:1 * BLOCK] = v2h
    sq_ref[1 * BLOCK:2 * BLOCK] = q2h
    sq_ref[2 * BLOCK:3 * BLOCK] = v2l
    sq_ref[3 * BLOCK:4 * BLOCK] = q2l
    ss = _dot(sq_ref[...], seg_ref[...])
    r = lax.rsqrt((ss[:2 * BLOCK] + ss[2 * BLOCK:]) * (1.0 / HEAD_DIM) + EPS)
    rh, rl = _split2(r)
    rr_ref[:2 * BLOCK] = rh
    rr_ref[2 * BLOCK:] = rl
    rb = _dot(rr_ref[...], exp_ref[...])
    rb = rb[:2 * BLOCK] + rb[2 * BLOCK:]
    vn = (v * rb[:BLOCK] * sgu_g_ref[...]).astype(BF16)
    qn = (q * rb[BLOCK:] * (qg_ref[...] * SCALE)).astype(BF16)

    segk = seg_ref[:KV_WIDTH, :]
    expk = exp_ref[:, :KV_WIDTH]
    k2h, k2l = _split2(k * k)
    rk = lax.rsqrt((_dot(k2h, segk) + _dot(k2l, segk)) * (1.0 / HEAD_DIM) + EPS)
    rkh, rkl = _split2(rk)
    kn = k * (_dot(rkh, expk) + _dot(rkl, expk)) * kg_ref[...]

    gates = []
    for p in range(N_LANE_BLOCKS):
        g2 = _dot(wm_ref[p], vn[:, p * LANES:(p + 1) * LANES])
        gates.append(jnp.where(lo_half, g2[:CHUNK], g2[CHUNK:]))
    gate = jnp.concatenate(gates, axis=-1) + sgu_b_ref[...]
    out_a = zu_ref[...].astype(F32) * gate
    o_ref[:, :SGU_WIDTH] = _rms_rows(out_a, ga_ref[...]).astype(BF16)

    lo_band = lax.broadcasted_iota(jnp.int32, (BAND, LANES), 1) < HEAD_DIM
    for j in range(N_KV_HEADS):
        p, half = divmod(j, HEADS_PER_LANE_BLOCK)
        keep = lo_band if half == 0 else jnp.logical_not(lo_band)
        k_same = jnp.where(keep, kn[:, p * LANES:(p + 1) * LANES], 0.0)
        v_same = jnp.where(keep, vv[:, p * LANES:(p + 1) * LANES], 0.0)
        k_other = pltpu.roll(k_same, HEAD_DIM, 1)
        v_other = pltpu.roll(v_same, HEAD_DIM, 1)
        k_slots = (k_same, k_other) if half == 0 else (k_other, k_same)
        v_slots = (v_same, v_other) if half == 0 else (v_other, v_same)
        for slot in range(2):
            rows = slice(slot * BAND, (slot + 1) * BAND)
            kband_ref[j, rows, :] = k_slots[slot].astype(BF16)
            vband_ref[j, rows, :LANES] = v_slots[slot].astype(BF16)

    qi = lax.broadcasted_iota(jnp.int32, (2 * BLOCK, 2 * BAND), 0) & (BLOCK - 1)
    kj = lax.broadcasted_iota(jnp.int32, (2 * BLOCK, 2 * BAND), 1) & (BAND - 1)
    dist = qi + BLOCK - kj
    first_key = jnp.where(i > 0, 0, BLOCK)
    valid = (dist >= 0) & (dist < WINDOW) & (kj >= first_key)
    top_rows = lax.broadcasted_iota(jnp.int32, (2 * BLOCK, 1), 0) < BLOCK
    lo_half2 = lax.broadcasted_iota(jnp.int32, (2 * BLOCK, LANES), 1) < HEAD_DIM

    out_blocks = []
    for j in range(N_KV_HEADS):
        q2 = jnp.concatenate([qn[:, (2 * j) * LANES:(2 * j + 1) * LANES],
                              qn[:, (2 * j + 1) * LANES:(2 * j + 2) * LANES]], axis=0)
        s = _dot_nt(q2, kband_ref[j])
        s = jnp.where(valid, s + bias_ref[j], NEG_INF)
        es, sink_terms = [], []
        for slot in range(2):
            sc = s[:, slot * BAND:(slot + 1) * BAND]
            sink = jnp.where(top_rows, sinks_ref[GROUP * j + slot],
                             sinks_ref[GROUP * j + HEADS_PER_LANE_BLOCK + slot])
            m = jnp.maximum(jnp.max(sc, axis=-1, keepdims=True), sink)
            es.append(jnp.exp(sc - m).astype(BF16))
            sink_terms.append(jnp.exp(sink - m))
        o = _dot(jnp.concatenate(es, axis=-1), vband_ref[j])
        den = o[:, LANES:] + jnp.where(lo_half2, sink_terms[0], sink_terms[1])
        outp = o[:, :LANES] * (1.0 / den)
        out_blocks += [outp[:BLOCK], outp[BLOCK:]]
    out_b = jnp.concatenate(out_blocks, axis=-1)
    o_ref[:, SGU_WIDTH:] = _rms_rows(out_b, gb_ref[...]).astype(BF16)


def _t5_bucket_table():
    qi = np.arange(BLOCK)[:, None]
    kj = np.arange(BAND)[None, :]
    n = np.maximum(qi + BLOCK - kj, 0)
    max_exact = NUM_BUCKETS // 2
    nf = np.maximum(n, 1).astype(np.float32)
    large = max_exact + (np.log(nf / np.float32(max_exact))
                         / np.float32(math.log(MAX_DISTANCE / max_exact))
                         * np.float32(NUM_BUCKETS - max_exact)).astype(np.int32)
    large = np.minimum(large, NUM_BUCKETS - 1)
    return np.where(n < max_exact, n, large).astype(np.int32)


def _head_segment_matrices():
    lane = np.arange(SGU_WIDTH)
    seg = (lane[:, None] // HEAD_DIM == np.arange(LANES)[None, :])
    return seg.astype(np.float32), seg.T.astype(np.float32)


def _cast_slabs(weights):
    steps = SEQ // BLOCK
    in_specs, out_specs, out_shapes, holds = [], [], [], []
    for w, layer in weights:
        _, rows, cols = w.shape
        hold = 1
        while (rows * hold) % steps or (rows * hold // steps) % BF16_SUBLANES:
            hold *= 2
        slab = rows * hold // steps
        in_specs.append(pl.BlockSpec((None, slab, cols),
                                     lambda i, layer=layer, hold=hold: (layer, i // hold, 0)))
        out_specs.append(pl.BlockSpec((slab, cols), lambda i, hold=hold: (i // hold, 0)))
        out_shapes.append(jax.ShapeDtypeStruct((rows, cols), BF16))
        holds.append(hold)
    return in_specs, out_specs, out_shapes, tuple(holds)


def _mixer(z, sinks, rel_bias, sgu_g, sgu_w, sgu_b, qg, kg, ga, gb, cast_weights):
    seg, exp = _head_segment_matrices()
    row = lambda n: pl.BlockSpec((1, n), lambda i: (0, 0))
    smem = pl.BlockSpec(memory_space=pltpu.SMEM)
    zcol = lambda w, c: pl.BlockSpec((BLOCK, w), lambda i: (i, c))
    kv_col = (2 * SGU_WIDTH + ATTN_WIDTH) // (2 * KV_WIDTH)
    cast_in, cast_out, cast_shapes, holds = _cast_slabs(cast_weights)
    cast_bytes = sum(2 * (4 + 2) * math.prod(s.block_shape[-2:]) for s in cast_in)
    return pl.pallas_call(
        functools.partial(_mixer_kernel, cast_holds=holds),
        out_shape=[jax.ShapeDtypeStruct((SEQ, D_MODEL), BF16)] + cast_shapes,
        grid=(SEQ // BLOCK,),
        in_specs=[
            smem, smem,
            zcol(SGU_WIDTH, 0), zcol(SGU_WIDTH, 1), zcol(ATTN_WIDTH, 2),
            zcol(2 * KV_WIDTH, kv_col),
            pl.BlockSpec((BLOCK, 2 * KV_WIDTH), lambda i: (jnp.maximum(i - 1, 0), kv_col)),
            row(SGU_WIDTH),
            pl.BlockSpec((N_SGU_HEADS, CHUNK, CHUNK), lambda i: (0, 0, 0)),
            pl.BlockSpec((CHUNK, SGU_WIDTH), lambda i: (0, 0)),
            row(ATTN_WIDTH), row(KV_WIDTH),
            pl.BlockSpec((BLOCK, BAND), lambda i: (0, 0)),
            pl.BlockSpec((SGU_WIDTH, LANES), lambda i: (0, 0)),
            pl.BlockSpec((LANES, SGU_WIDTH), lambda i: (0, 0)),
            row(SGU_WIDTH), row(ATTN_WIDTH),
        ] + cast_in,
        out_specs=[pl.BlockSpec((BLOCK, D_MODEL), lambda i: (i, 0))] + cast_out,
        scratch_shapes=[
            pltpu.VMEM((N_LANE_BLOCKS, 2 * CHUNK, CHUNK), BF16),
            pltpu.VMEM((N_KV_HEADS, 2 * BLOCK, 2 * BAND), F32),
            pltpu.VMEM((N_KV_HEADS, 2 * BAND, LANES), BF16),
            pltpu.VMEM((N_KV_HEADS, 2 * BAND, 2 * LANES), BF16),
            pltpu.VMEM((4 * BLOCK, SGU_WIDTH), BF16),
            pltpu.VMEM((4 * BLOCK, LANES), BF16),
        ],
        compiler_params=pltpu.CompilerParams(
            dimension_semantics=("arbitrary",),
            vmem_limit_bytes=MIXER_VMEM_BYTES + cast_bytes),
        name="mixer",
    )(sinks, rel_bias, z, z, z, z, z, sgu_g, sgu_w, sgu_b, qg, kg,
      jnp.asarray(_t5_bucket_table()), jnp.asarray(seg, BF16), jnp.asarray(exp, BF16),
      ga, gb, *[w for w, _ in cast_weights])


def _ffn_kernel(x_ref, mix_ref, wo_ref, g_ref, wg_ref, wu_ref, wd_ref, o_ref, h_ref):
    f = pl.program_id(1)

    @pl.when(f == 0)
    def _():
        x1 = x_ref[...] + _dot(mix_ref[...], wo_ref[...])
        o_ref[...] = x1
        h_ref[...] = _rms_rows(x1, g_ref[...]).astype(BF16)

    h = h_ref[...]
    gate = _dot(h, wg_ref[...])
    up = _dot(h, wu_ref[...])
    act = (gate * jax.nn.sigmoid(gate) * up).astype(BF16)
    o_ref[...] += _dot(act, wd_ref[...])


def _ffn(x, mixed, w_out, g, w_gate, w_up, w_down):
    vmem = (4 * FFN_TM * D_MODEL * 4 + 2 * FFN_TM * D_MODEL * 2
            + 2 * D_MODEL * D_MODEL * 2 + 6 * D_MODEL * FFN_TF * 2
            + FFN_TM * D_MODEL * 2 + 4 * FFN_TM * FFN_TF * 4 + 4 * MIB)
    return pl.pallas_call(
        _ffn_kernel,
        out_shape=jax.ShapeDtypeStruct((SEQ, D_MODEL), F32),
        grid=(SEQ // FFN_TM, D_FF // FFN_TF),
        in_specs=[
            pl.BlockSpec((FFN_TM, D_MODEL), lambda i, f: (i, 0)),
            pl.BlockSpec((FFN_TM, D_MODEL), lambda i, f: (i, 0)),
            pl.BlockSpec((D_MODEL, D_MODEL), lambda i, f: (0, 0)),
            pl.BlockSpec((1, D_MODEL), lambda i, f: (0, 0)),
            pl.BlockSpec((D_MODEL, FFN_TF), lambda i, f: (0, f)),
            pl.BlockSpec((D_MODEL, FFN_TF), lambda i, f: (0, f)),
            pl.BlockSpec((FFN_TF, D_MODEL), lambda i, f: (f, 0)),
        ],
        out_specs=pl.BlockSpec((FFN_TM, D_MODEL), lambda i, f: (i, 0)),
        scratch_shapes=[pltpu.VMEM((FFN_TM, D_MODEL), BF16)],
        compiler_params=pltpu.CompilerParams(
            dimension_semantics=("arbitrary", "arbitrary"),
            vmem_limit_bytes=vmem),
        name="ffn",
    )(x, mixed, w_out, g, w_gate, w_up, w_down)


def kernel(x, rel_bias, norm1_g, w_in, sgu_norm_g, sgu_w, sgu_b, q_norm_g, k_norm_g,
           sinks, out_norm_a, out_norm_b, w_out, norm2_g, w_gate, w_up, w_down):
    assert x.shape == (1, SEQ, D_MODEL)
    h = x.reshape(SEQ, D_MODEL)
    w_in_b = w_in[0].astype(BF16)
    for l in range(DEPTH):
        z = _in_proj(h, norm1_g[l].reshape(1, D_MODEL), w_in_b)
        cast_weights = [(w_out, l), (w_gate, l), (w_up, l), (w_down, l)]
        if l + 1 < DEPTH:
            cast_weights.append((w_in, l + 1))
        mixed, w_out_b, w_gate_b, w_up_b, w_down_b, *w_in_next = _mixer(
            z, sinks[l], rel_bias,
            sgu_norm_g[l].reshape(1, SGU_WIDTH), sgu_w[l],
            jnp.repeat(sgu_b[l].T, HEAD_DIM, axis=1),
            jnp.tile(q_norm_g[l], N_Q_HEADS).reshape(1, ATTN_WIDTH),
            jnp.tile(k_norm_g[l], N_KV_HEADS).reshape(1, KV_WIDTH),
            out_norm_a[l].reshape(1, SGU_WIDTH), out_norm_b[l].reshape(1, ATTN_WIDTH),
            cast_weights)
        if w_in_next:
            w_in_b, = w_in_next
        h = _ffn(h, mixed, w_out_b, norm2_g[l].reshape(1, D_MODEL),
                 w_gate_b, w_up_b, w_down_b)
    return h.reshape(1, SEQ, D_MODEL)
```

```python
import functools
import math

import numpy as np
import jax
import jax.numpy as jnp
from jax import lax
from jax.experimental import pallas as pl
from jax.experimental.pallas import tpu as pltpu

D_MODEL = 2048
SEQ = 8192
DEPTH = 4
HEAD_DIM = 64
SGU_WIDTH = D_MODEL // 2
N_SGU_HEADS = SGU_WIDTH // HEAD_DIM
CHUNK = 128
ATTN_WIDTH = D_MODEL - SGU_WIDTH
N_Q_HEADS = ATTN_WIDTH // HEAD_DIM
N_KV_HEADS = 4
GROUP = N_Q_HEADS // N_KV_HEADS
KV_WIDTH = N_KV_HEADS * HEAD_DIM
WINDOW = 128
BLOCK = 128
BAND = 2 * BLOCK
NUM_BUCKETS = 32
MAX_DISTANCE = 128
IN_WIDTH = 2 * SGU_WIDTH + ATTN_WIDTH + 2 * KV_WIDTH
D_FF = -(-8 * D_MODEL // (3 * 256)) * 256
EPS = 1e-6
NEG_INF = -1e30
SCALE = 1.0 / math.sqrt(HEAD_DIM)

LANES = 128
BF16_SUBLANES = 16
HEADS_PER_LANE_BLOCK = LANES // HEAD_DIM
N_LANE_BLOCKS = SGU_WIDTH // LANES
MIB = 1024 * 1024

F32 = jnp.float32
BF16 = jnp.bfloat16

IN_TM = 512
IN_TN = 512
N_GELU_TILES = (2 * SGU_WIDTH) // IN_TN
FFN_TM = 512
FFN_TF = 512
MIXER_VMEM_BYTES = 16 * MIB

assert CHUNK == BLOCK == LANES and HEADS_PER_LANE_BLOCK == 2 and GROUP == 4


def _rms_rows(xf, g):
    ms = jnp.mean(xf * xf, axis=-1, keepdims=True)
    return xf * lax.rsqrt(ms + EPS) * g


def _split2(x):
    hi = x.astype(BF16)
    return hi, (x - hi.astype(F32)).astype(BF16)


def _dot(a, b):
    return jnp.dot(a, b, preferred_element_type=F32)


def _dot_nt(a, b):
    return lax.dot_general(a, b, (((1,), (1,)), ((), ())), preferred_element_type=F32)


def _in_proj_kernel(x_ref, g_ref, w_ref, o_ref):
    h = _rms_rows(x_ref[...], g_ref[...]).astype(BF16)
    for c in range(IN_WIDTH // IN_TN):
        cols = slice(c * IN_TN, (c + 1) * IN_TN)
        z = _dot(h, w_ref[:, cols])
        if c < N_GELU_TILES:
            z = 0.5 * z * (1.0 + lax.erf(z * (1.0 / math.sqrt(2.0))))
        o_ref[:, cols] = z.astype(BF16)


def _in_proj(x, g, w):
    vmem = (2 * IN_TM * D_MODEL * 4 + 2 * D_MODEL * IN_WIDTH * 2 + 2 * IN_TM * IN_WIDTH * 2
            + IN_TM * D_MODEL * 2 + 4 * IN_TM * IN_TN * 4 + 4 * MIB)
    return pl.pallas_call(
        _in_proj_kernel,
        out_shape=jax.ShapeDtypeStruct((SEQ, IN_WIDTH), BF16),
        grid=(SEQ // IN_TM,),
        in_specs=[
            pl.BlockSpec((IN_TM, D_MODEL), lambda i: (i, 0)),
            pl.BlockSpec((1, D_MODEL), lambda i: (0, 0)),
            pl.BlockSpec((D_MODEL, IN_WIDTH), lambda i: (0, 0)),
        ],
        out_specs=pl.BlockSpec((IN_TM, IN_WIDTH), lambda i: (i, 0)),
        compiler_params=pltpu.CompilerParams(
            dimension_semantics=("arbitrary",),
            vmem_limit_bytes=vmem),
        name="in_proj",
    )(x, g, w)


def _mixer_kernel(sinks_ref, relb_ref,
                  zu_ref, zv_ref, zq_ref, zkv_ref, zkvp_ref,
                  sgu_g_ref, sgu_w_ref, sgu_b_ref, qg_ref, kg_ref,
                  bucket_ref, seg_ref, exp_ref, ga_ref, gb_ref,
                  *rest, cast_holds):
    n_cast = len(cast_holds)
    cast_in = rest[:n_cast]
    o_ref = rest[n_cast]
    cast_out = rest[n_cast + 1:2 * n_cast + 1]
    wm_ref, bias_ref, kband_ref, vband_ref, sq_ref, rr_ref = rest[2 * n_cast + 1:]
    i = pl.program_id(0)

    for src, dst, hold in zip(cast_in, cast_out, cast_holds):
        if hold == 1:
            dst[...] = src[...].astype(BF16)
        else:
            @pl.when(i % hold == 0)
            def _(src=src, dst=dst):
                dst[...] = src[...].astype(BF16)

    lane = lax.broadcasted_iota(jnp.int32, (BLOCK, LANES), 1)
    lo_half = lane < HEAD_DIM

    @pl.when(i == 0)
    def _():
        t_idx = lax.broadcasted_iota(jnp.int32, (CHUNK, CHUNK), 0)
        s_idx = lax.broadcasted_iota(jnp.int32, (CHUNK, CHUNK), 1)
        keep = s_idx <= t_idx

        def mask_pair(p, c):
            wm_ref[p, :CHUNK, :] = jnp.where(keep, sgu_w_ref[2 * p], 0.0).astype(BF16)
            wm_ref[p, CHUNK:, :] = jnp.where(keep, sgu_w_ref[2 * p + 1], 0.0).astype(BF16)
            return c
        lax.fori_loop(0, N_LANE_BLOCKS, mask_pair, 0)

        bucket = bucket_ref[...]
        for rb in range(2):
            for slot in range(2):
                def bias_head(j, c, rb=rb, slot=slot):
                    h = GROUP * j + HEADS_PER_LANE_BLOCK * rb + slot
                    acc = jnp.zeros((BLOCK, BAND), F32)
                    for b in range(NUM_BUCKETS):
                        acc = jnp.where(bucket == b, relb_ref[b, h], acc)
                    bias_ref[j, rb * BLOCK:(rb + 1) * BLOCK, slot * BAND:(slot + 1) * BAND] = acc
                    return c
                lax.fori_loop(0, N_KV_HEADS, bias_head, 0)

        row = lax.broadcasted_iota(jnp.int32, (2 * BAND, 2 * LANES), 0)
        col = lax.broadcasted_iota(jnp.int32, (2 * BAND, 2 * LANES), 1)
        ones = (((row < BAND) & (col >= LANES) & (col < LANES + HEAD_DIM))
                | ((row >= BAND) & (col >= LANES + HEAD_DIM)))
        pattern = jnp.where(ones, 1.0, 0.0).astype(BF16)
        for j in range(N_KV_HEADS):
            vband_ref[j] = pattern

    v = zv_ref[...].astype(F32)
    q = zq_ref[...].astype(F32)
    kv = jnp.concatenate([zkvp_ref[...], zkv_ref[...]], axis=0).astype(F32)
    k = kv[:, :KV_WIDTH]
    vv = kv[:, KV_WIDTH:]

    v2h, v2l = _split2(v * v)
    q2h, q2l = _split2(q * q)
    sq_ref[0 * BLOCK:1 * BLOCK] = v2h
    sq_ref[1 * BLOCK:2 * BLOCK] = q2h
    sq_ref[2 * BLOCK:3 * BLOCK] = v2l
    sq_ref[3 * BLOCK:4 * BLOCK] = q2l
    ss = _dot(sq_ref[...], seg_ref[...])
    r = lax.rsqrt((ss[:2 * BLOCK] + ss[2 * BLOCK:]) * (1.0 / HEAD_DIM) + EPS)
    rh, rl = _split2(r)
    rr_ref[:2 * BLOCK] = rh
    rr_ref[2 * BLOCK:] = rl
    rb = _dot(rr_ref[...], exp_ref[...])
    rb = rb[:2 * BLOCK] + rb[2 * BLOCK:]
    vn = (v * rb[:BLOCK] * sgu_g_ref[...]).astype(BF16)
    qn = (q * rb[BLOCK:] * (qg_ref[...] * SCALE)).astype(BF16)

    segk = seg_ref[:KV_WIDTH, :]
    expk = exp_ref[:, :KV_WIDTH]
    k2h, k2l = _split2(k * k)
    rk = lax.rsqrt((_dot(k2h, segk) + _dot(k2l, segk)) * (1.0 / HEAD_DIM) + EPS)
    rkh, rkl = _split2(rk)
    kn = k * (_dot(rkh, expk) + _dot(rkl, expk)) * kg_ref[...]

    gates = []
    for p in range(N_LANE_BLOCKS):
        g2 = _dot(wm_ref[p], vn[:, p * LANES:(p + 1) * LANES])
        gates.append(jnp.where(lo_half, g2[:CHUNK], g2[CHUNK:]))
    gate = jnp.concatenate(gates, axis=-1) + sgu_b_ref[...]
    out_a = zu_ref[...].astype(F32) * gate
    o_ref[:, :SGU_WIDTH] = _rms_rows(out_a, ga_ref[...]).astype(BF16)

    lo_band = lax.broadcasted_iota(jnp.int32, (BAND, LANES), 1) < HEAD_DIM
    for j in range(N_KV_HEADS):
        p, half = divmod(j, HEADS_PER_LANE_BLOCK)
        keep = lo_band if half == 0 else jnp.logical_not(lo_band)
        k_same = jnp.where(keep, kn[:, p * LANES:(p + 1) * LANES], 0.0)
        v_same = jnp.where(keep, vv[:, p * LANES:(p + 1) * LANES], 0.0)
        k_other = pltpu.roll(k_same, HEAD_DIM, 1)
        v_other = pltpu.roll(v_same, HEAD_DIM, 1)
        k_slots = (k_same, k_other) if half == 0 else (k_other, k_same)
        v_slots = (v_same, v_other) if half == 0 else (v_other, v_same)
        for slot in range(2):
            rows = slice(slot * BAND, (slot + 1) * BAND)
            kband_ref[j, rows, :] = k_slots[slot].astype(BF16)
            vband_ref[j, rows, :LANES] = v_slots[slot].astype(BF16)

    qi = lax.broadcasted_iota(jnp.int32, (2 * BLOCK, 2 * BAND), 0) & (BLOCK - 1)
    kj = lax.broadcasted_iota(jnp.int32, (2 * BLOCK, 2 * BAND), 1) & (BAND - 1)
    dist = qi + BLOCK - kj
    first_key = jnp.where(i > 0, 0, BLOCK)
    valid = (dist >= 0) & (dist < WINDOW) & (kj >= first_key)
    top_rows = lax.broadcasted_iota(jnp.int32, (2 * BLOCK, 1), 0) < BLOCK
    lo_half2 = lax.broadcasted_iota(jnp.int32, (2 * BLOCK, LANES), 1) < HEAD_DIM

    out_blocks = []
    for j in range(N_KV_HEADS):
        q2 = jnp.concatenate([qn[:, (2 * j) * LANES:(2 * j + 1) * LANES],
                              qn[:, (2 * j + 1) * LANES:(2 * j + 2) * LANES]], axis=0)
        s = _dot_nt(q2, kband_ref[j])
        s = jnp.where(valid, s + bias_ref[j], NEG_INF)
        es, sink_terms = [], []
        for slot in range(2):
            sc = s[:, slot * BAND:(slot + 1) * BAND]
            sink = jnp.where(top_rows, sinks_ref[GROUP * j + slot],
                             sinks_ref[GROUP * j + HEADS_PER_LANE_BLOCK + slot])
            m = jnp.maximum(jnp.max(sc, axis=-1, keepdims=True), sink)
            es.append(jnp.exp(sc - m).astype(BF16))
            sink_terms.append(jnp.exp(sink - m))
        o = _dot(jnp.concatenate(es, axis=-1), vband_ref[j])
        den = o[:, LANES:] + jnp.where(lo_half2, sink_terms[0], sink_terms[1])
        outp = o[:, :LANES] * (1.0 / den)
        out_blocks += [outp[:BLOCK], outp[BLOCK:]]
    out_b = jnp.concatenate(out_blocks, axis=-1)
    o_ref[:, SGU_WIDTH:] = _rms_rows(out_b, gb_ref[...]).astype(BF16)


def _t5_bucket_table():
    qi = np.arange(BLOCK)[:, None]
    kj = np.arange(BAND)[None, :]
    n = np.maximum(qi + BLOCK - kj, 0)
    max_exact = NUM_BUCKETS // 2
    nf = np.maximum(n, 1).astype(np.float32)
    large = max_exact + (np.log(nf / np.float32(max_exact))
                         / np.float32(math.log(MAX_DISTANCE / max_exact))
                         * np.float32(NUM_BUCKETS - max_exact)).astype(np.int32)
    large = np.minimum(large, NUM_BUCKETS - 1)
    return np.where(n < max_exact, n, large).astype(np.int32)


def _head_segment_matrices():
    lane = np.arange(SGU_WIDTH)
    seg = (lane[:, None] // HEAD_DIM == np.arange(LANES)[None, :])
    return seg.astype(np.float32), seg.T.astype(np.float32)


def _cast_slabs(weights):
    steps = SEQ // BLOCK
    in_specs, out_specs, out_shapes, holds = [], [], [], []
    for w, layer in weights:
        _, rows, cols = w.shape
        hold = 1
        while (rows * hold) % steps or (rows * hold // steps) % BF16_SUBLANES:
            hold *= 2
        slab = rows * hold // steps
        in_specs.append(pl.BlockSpec((None, slab, cols),
                                     lambda i, layer=layer, hold=hold: (layer, i // hold, 0)))
        out_specs.append(pl.BlockSpec((slab, cols), lambda i, hold=hold: (i // hold, 0)))
        out_shapes.append(jax.ShapeDtypeStruct((rows, cols), BF16))
        holds.append(hold)
    return in_specs, out_specs, out_shapes, tuple(holds)


def _mixer(z, sinks, rel_bias, sgu_g, sgu_w, sgu_b, qg, kg, ga, gb, cast_weights):
    seg, exp = _head_segment_matrices()
    row = lambda n: pl.BlockSpec((1, n), lambda i: (0, 0))
    smem = pl.BlockSpec(memory_space=pltpu.SMEM)
    zcol = lambda w, c: pl.BlockSpec((BLOCK, w), lambda i: (i, c))
    kv_col = (2 * SGU_WIDTH + ATTN_WIDTH) // (2 * KV_WIDTH)
    cast_in, cast_out, cast_shapes, holds = _cast_slabs(cast_weights)
    cast_bytes = sum(2 * (4 + 2) * math.prod(s.block_shape[-2:]) for s in cast_in)
    return pl.pallas_call(
        functools.partial(_mixer_kernel, cast_holds=holds),
        out_shape=[jax.ShapeDtypeStruct((SEQ, D_MODEL), BF16)] + cast_shapes,
        grid=(SEQ // BLOCK,),
        in_specs=[
            smem, smem,
            zcol(SGU_WIDTH, 0), zcol(SGU_WIDTH, 1), zcol(ATTN_WIDTH, 2),
            zcol(2 * KV_WIDTH, kv_col),
            pl.BlockSpec((BLOCK, 2 * KV_WIDTH), lambda i: (jnp.maximum(i - 1, 0), kv_col)),
            row(SGU_WIDTH),
            pl.BlockSpec((N_SGU_HEADS, CHUNK, CHUNK), lambda i: (0, 0, 0)),
            pl.BlockSpec((CHUNK, SGU_WIDTH), lambda i: (0, 0)),
            row(ATTN_WIDTH), row(KV_WIDTH),
            pl.BlockSpec((BLOCK, BAND), lambda i: (0, 0)),
            pl.BlockSpec((SGU_WIDTH, LANES), lambda i: (0, 0)),
            pl.BlockSpec((LANES, SGU_WIDTH), lambda i: (0, 0)),
            row(SGU_WIDTH), row(ATTN_WIDTH),
        ] + cast_in,
        out_specs=[pl.BlockSpec((BLOCK, D_MODEL), lambda i: (i, 0))] + cast_out,
        scratch_shapes=[
            pltpu.VMEM((N_LANE_BLOCKS, 2 * CHUNK, CHUNK), BF16),
            pltpu.VMEM((N_KV_HEADS, 2 * BLOCK, 2 * BAND), F32),
            pltpu.VMEM((N_KV_HEADS, 2 * BAND, LANES), BF16),
            pltpu.VMEM((N_KV_HEADS, 2 * BAND, 2 * LANES), BF16),
            pltpu.VMEM((4 * BLOCK, SGU_WIDTH), BF16),
            pltpu.VMEM((4 * BLOCK, LANES), BF16),
        ],
        compiler_params=pltpu.CompilerParams(
            dimension_semantics=("arbitrary",),
            vmem_limit_bytes=MIXER_VMEM_BYTES + cast_bytes),
        name="mixer",
    )(sinks, rel_bias, z, z, z, z, z, sgu_g, sgu_w, sgu_b, qg, kg,
      jnp.asarray(_t5_bucket_table()), jnp.asarray(seg, BF16), jnp.asarray(exp, BF16),
      ga, gb, *[w for w, _ in cast_weights])


def _ffn_kernel(x_ref, mix_ref, wo_ref, g_ref, wg_ref, wu_ref, wd_ref, o_ref, h_ref):
    f = pl.program_id(1)

    @pl.when(f == 0)
    def _():
        x1 = x_ref[...] + _dot(mix_ref[...], wo_ref[...])
        o_ref[...] = x1
        h_ref[...] = _rms_rows(x1, g_ref[...]).astype(BF16)

    h = h_ref[...]
    gate = _dot(h, wg_ref[...])
    up = _dot(h, wu_ref[...])
    act = (gate * jax.nn.sigmoid(gate) * up).astype(BF16)
    o_ref[...] += _dot(act, wd_ref[...])


def _ffn(x, mixed, w_out, g, w_gate, w_up, w_down):
    vmem = (4 * FFN_TM * D_MODEL * 4 + 2 * FFN_TM * D_MODEL * 2
            + 2 * D_MODEL * D_MODEL * 2 + 6 * D_MODEL * FFN_TF * 2
            + FFN_TM * D_MODEL * 2 + 4 * FFN_TM * FFN_TF * 4 + 4 * MIB)
    return pl.pallas_call(
        _ffn_kernel,
        out_shape=jax.ShapeDtypeStruct((SEQ, D_MODEL), F32),
        grid=(SEQ // FFN_TM, D_FF // FFN_TF),
        in_specs=[
            pl.BlockSpec((FFN_TM, D_MODEL), lambda i, f: (i, 0)),
            pl.BlockSpec((FFN_TM, D_MODEL), lambda i, f: (i, 0)),
            pl.BlockSpec((D_MODEL, D_MODEL), lambda i, f: (0, 0)),
            pl.BlockSpec((1, D_MODEL), lambda i, f: (0, 0)),
            pl.BlockSpec((D_MODEL, FFN_TF), lambda i, f: (0, f)),
            pl.BlockSpec((D_MODEL, FFN_TF), lambda i, f: (0, f)),
            pl.BlockSpec((FFN_TF, D_MODEL), lambda i, f: (f, 0)),
        ],
        out_specs=pl.BlockSpec((FFN_TM, D_MODEL), lambda i, f: (i, 0)),
        scratch_shapes=[pltpu.VMEM((FFN_TM, D_MODEL), BF16)],
        compiler_params=pltpu.CompilerParams(
            dimension_semantics=("arbitrary", "arbitrary"),
            vmem_limit_bytes=vmem),
        name="ffn",
    )(x, mixed, w_out, g, w_gate, w_up, w_down)


def kernel(x, rel_bias, norm1_g, w_in, sgu_norm_g, sgu_w, sgu_b, q_norm_g, k_norm_g,
           sinks, out_norm_a, out_norm_b, w_out, norm2_g, w_gate, w_up, w_down):
    assert x.shape == (1, SEQ, D_MODEL)
    h = x.reshape(SEQ, D_MODEL)
    w_in_b = w_in[0].astype(BF16)
    for l in range(DEPTH):
        z = _in_proj(h, norm1_g[l].reshape(1, D_MODEL), w_in_b)
        cast_weights = [(w_out, l), (w_gate, l), (w_up, l), (w_down, l)]
        if l + 1 < DEPTH:
            cast_weights.append((w_in, l + 1))
        mixed, w_out_b, w_gate_b, w_up_b, w_down_b, *w_in_next = _mixer(
            z, sinks[l], rel_bias,
            sgu_norm_g[l].reshape(1, SGU_WIDTH), sgu_w[l],
            jnp.repeat(sgu_b[l].T, HEAD_DIM, axis=1),
            jnp.tile(q_norm_g[l], N_Q_HEADS).reshape(1, ATTN_WIDTH),
            jnp.tile(k_norm_g[l], N_KV_HEADS).reshape(1, KV_WIDTH),
            out_norm_a[l].reshape(1, SGU_WIDTH), out_norm_b[l].reshape(1, ATTN_WIDTH),
            cast_weights)
        if w_in_next:
            w_in_b, = w_in_next
        h = _ffn(h, mixed, w_out_b, norm2_g[l].reshape(1, D_MODEL),
                 w_gate_b, w_up_b, w_down_b)
    return h.reshape(1, SEQ, D_MODEL)
```

```python
import functools
import math

import numpy as np
import jax
import jax.numpy as jnp
from jax import lax
from jax.experimental import pallas as pl
from jax.experimental.pallas import tpu as pltpu

D_MODEL = 2048
SEQ = 8192
DEPTH = 4
HEAD_DIM = 64
SGU_WIDTH = D_MODEL // 2
N_SGU_HEADS = SGU_WIDTH // HEAD_DIM
CHUNK = 128
ATTN_WIDTH = D_MODEL - SGU_WIDTH
N_Q_HEADS = ATTN_WIDTH // HEAD_DIM
N_KV_HEADS = 4
GROUP = N_Q_HEADS // N_KV_HEADS
KV_WIDTH = N_KV_HEADS * HEAD_DIM
WINDOW = 128
BLOCK = 128
BAND = 2 * BLOCK
NUM_BUCKETS = 32
MAX_DISTANCE = 128
IN_WIDTH = 2 * SGU_WIDTH + ATTN_WIDTH + 2 * KV_WIDTH
D_FF = -(-8 * D_MODEL // (3 * 256)) * 256
EPS = 1e-6
NEG_INF = -1e30
SCALE = 1.0 / math.sqrt(HEAD_DIM)
LOG2E = math.log2(math.e)
F32_MAX = float(np.finfo(np.float32).max)

LANES = 128
BF16_SUBLANES = 16
HEADS_PER_LANE_BLOCK = LANES // HEAD_DIM
N_LANE_BLOCKS = SGU_WIDTH // LANES
MIB = 1024 * 1024

F32 = jnp.float32
BF16 = jnp.bfloat16

IN_TM = 512
IN_TN = 512
N_GELU_TILES = (2 * SGU_WIDTH) // IN_TN
FFN_TM = 512
FFN_TF = 512
MIXER_VMEM_BYTES = 20 * MIB

assert CHUNK == BLOCK == LANES and HEADS_PER_LANE_BLOCK == 2 and GROUP == 4


def _rms_rows(xf, g):
    ms = jnp.mean(xf * xf, axis=-1, keepdims=True)
    return xf * lax.rsqrt(ms + EPS) * g


def _split2(x):
    hi = x.astype(BF16)
    return hi, (x - hi.astype(F32)).astype(BF16)


def _dot(a, b):
    return jnp.dot(a, b, preferred_element_type=F32)


def _dot_nt(a, b):
    return lax.dot_general(a, b, (((1,), (1,)), ((), ())), preferred_element_type=F32)


def _in_proj_kernel(x_ref, g_ref, w_ref, o_ref):
    h = _rms_rows(x_ref[...], g_ref[...]).astype(BF16)
    for c in range(IN_WIDTH // IN_TN):
        cols = slice(c * IN_TN, (c + 1) * IN_TN)
        z = _dot(h, w_ref[:, cols])
        if c < N_GELU_TILES:
            z = 0.5 * z * (1.0 + lax.erf(z * (1.0 / math.sqrt(2.0))))
        o_ref[:, cols] = z.astype(BF16)


def _in_proj(x, g, w):
    vmem = (2 * IN_TM * D_MODEL * 4 + 2 * D_MODEL * IN_WIDTH * 2 + 2 * IN_TM * IN_WIDTH * 2
            + IN_TM * D_MODEL * 2 + 4 * IN_TM * IN_TN * 4 + 4 * MIB)
    return pl.pallas_call(
        _in_proj_kernel,
        out_shape=jax.ShapeDtypeStruct((SEQ, IN_WIDTH), BF16),
        grid=(SEQ // IN_TM,),
        in_specs=[
            pl.BlockSpec((IN_TM, D_MODEL), lambda i: (i, 0)),
            pl.BlockSpec((1, D_MODEL), lambda i: (0, 0)),
            pl.BlockSpec((D_MODEL, IN_WIDTH), lambda i: (0, 0)),
        ],
        out_specs=pl.BlockSpec((IN_TM, IN_WIDTH), lambda i: (i, 0)),
        compiler_params=pltpu.CompilerParams(
            dimension_semantics=("arbitrary",),
            vmem_limit_bytes=vmem),
        name="in_proj",
    )(x, g, w)


def _mixer_kernel(sinks_ref, relb_ref,
                  zu_ref, zv_ref, zq_ref, zkv_ref, zkvp_ref,
                  sgu_g_ref, sgu_w_ref, sgu_b_ref, qg_ref, kg_ref,
                  bucket_ref, seg_ref, exp_ref, ga_ref, gb_ref,
                  *rest, cast_holds):
    n_cast = len(cast_holds)
    cast_in = rest[:n_cast]
    o_ref = rest[n_cast]
    cast_out = rest[n_cast + 1:2 * n_cast + 1]
    wm_ref, bias_ref, cap_ref, kband_ref, vband_ref, sq_ref, rr_ref = rest[2 * n_cast + 1:]
    i = pl.program_id(0)

    for src, dst, hold in zip(cast_in, cast_out, cast_holds):
        if hold == 1:
            dst[...] = src[...].astype(BF16)
        else:
            @pl.when(i % hold == 0)
            def _(src=src, dst=dst):
                dst[...] = src[...].astype(BF16)

    lane = lax.broadcasted_iota(jnp.int32, (BLOCK, LANES), 1)
    lo_half = lane < HEAD_DIM

    @pl.when(i == 0)
    def _():
        t_idx = lax.broadcasted_iota(jnp.int32, (CHUNK, CHUNK), 0)
        s_idx = lax.broadcasted_iota(jnp.int32, (CHUNK, CHUNK), 1)
        keep = s_idx <= t_idx

        def mask_pair(p, c):
            wm_ref[p, :CHUNK, :] = jnp.where(keep, sgu_w_ref[2 * p], 0.0).astype(BF16)
            wm_ref[p, CHUNK:, :] = jnp.where(keep, sgu_w_ref[2 * p + 1], 0.0).astype(BF16)
            return c
        lax.fori_loop(0, N_LANE_BLOCKS, mask_pair, 0)

        bucket = bucket_ref[...]
        qi = lax.broadcasted_iota(jnp.int32, (BLOCK, BAND), 0)
        kj = lax.broadcasted_iota(jnp.int32, (BLOCK, BAND), 1)
        dist = qi + BLOCK - kj
        in_window = (dist >= 0) & (dist < WINDOW)
        oks = (in_window, in_window & (kj >= BLOCK))
        for rb in range(2):
            rows = slice(rb * BLOCK, (rb + 1) * BLOCK)
            for slot in range(2):
                cols = slice(slot * BAND, (slot + 1) * BAND)
                for first, ok in enumerate(oks):
                    cap_ref[first, rows, cols] = jnp.where(ok, F32_MAX, NEG_INF)

                def bias_head(j, c, rows=rows, cols=cols, rb=rb, slot=slot):
                    h = GROUP * j + HEADS_PER_LANE_BLOCK * rb + slot
                    acc = jnp.zeros((BLOCK, BAND), F32)
                    for b in range(NUM_BUCKETS):
                        acc = jnp.where(bucket == b, relb_ref[b, h] * LOG2E, acc)
                    for first, ok in enumerate(oks):
                        bias_ref[first * N_KV_HEADS + j, rows, cols] = jnp.where(ok, acc, 0.0)
                    return c
                lax.fori_loop(0, N_KV_HEADS, bias_head, 0)

        row = lax.broadcasted_iota(jnp.int32, (2 * BAND, 2 * LANES), 0)
        col = lax.broadcasted_iota(jnp.int32, (2 * BAND, 2 * LANES), 1)
        ones = (((row < BAND) & (col >= LANES) & (col < LANES + HEAD_DIM))
                | ((row >= BAND) & (col >= LANES + HEAD_DIM)))
        pattern = jnp.where(ones, 1.0, 0.0).astype(BF16)
        for j in range(N_KV_HEADS):
            vband_ref[j] = pattern

    vb = zv_ref[...]
    qb = zq_ref[...]
    kvb = jnp.concatenate([zkvp_ref[...], zkv_ref[...]], axis=0)
    kb = kvb[:, :KV_WIDTH]
    vv = kvb[:, KV_WIDTH:].astype(F32)

    sq_ref[:BLOCK] = vb * vb
    sq_ref[BLOCK:] = qb * qb
    ss = _dot(sq_ref[...], seg_ref[...])
    rh, rl = _split2(lax.rsqrt(ss * (1.0 / HEAD_DIM) + EPS))
    rr_ref[:2 * BLOCK] = rh
    rr_ref[2 * BLOCK:] = rl
    rb = _dot(rr_ref[...], exp_ref[...])
    rb = rb[:2 * BLOCK] + rb[2 * BLOCK:]
    vn = (vb.astype(F32) * rb[:BLOCK] * sgu_g_ref[...]).astype(BF16)
    qn = (qb.astype(F32) * rb[BLOCK:] * (qg_ref[...] * (SCALE * LOG2E))).astype(BF16)

    segk = seg_ref[:KV_WIDTH, :]
    expk = exp_ref[:, :KV_WIDTH]
    rkh, rkl = _split2(lax.rsqrt(_dot(kb * kb, segk) * (1.0 / HEAD_DIM) + EPS))
    kn = kb.astype(F32) * (_dot(rkh, expk) + _dot(rkl, expk)) * kg_ref[...]

    gates = []
    for p in range(N_LANE_BLOCKS):
        g2 = _dot(wm_ref[p], vn[:, p * LANES:(p + 1) * LANES])
        gates.append(jnp.where(lo_half, g2[:CHUNK], g2[CHUNK:]))
    gate = jnp.concatenate(gates, axis=-1) + sgu_b_ref[...]
    out_a = zu_ref[...].astype(F32) * gate
    o_ref[:, :SGU_WIDTH] = _rms_rows(out_a, ga_ref[...]).astype(BF16)

    lo_band = lax.broadcasted_iota(jnp.int32, (BAND, LANES), 1) < HEAD_DIM
    for j in range(N_KV_HEADS):
        p, half = divmod(j, HEADS_PER_LANE_BLOCK)
        keep = lo_band if half == 0 else jnp.logical_not(lo_band)
        k_same = jnp.where(keep, kn[:, p * LANES:(p + 1) * LANES], 0.0)
        v_same = jnp.where(keep, vv[:, p * LANES:(p + 1) * LANES], 0.0)
        k_other = pltpu.roll(k_same, HEAD_DIM, 1)
        v_other = pltpu.roll(v_same, HEAD_DIM, 1)
        k_slots = (k_same, k_other) if half == 0 else (k_other, k_same)
        v_slots = (v_same, v_other) if half == 0 else (v_other, v_same)
        for slot in range(2):
            rows = slice(slot * BAND, (slot + 1) * BAND)
            kband_ref[j, rows, :] = k_slots[slot].astype(BF16)
            vband_ref[j, rows, :LANES] = v_slots[slot].astype(BF16)

    tables = jnp.where(i == 0, 1, 0)
    top_rows = lax.broadcasted_iota(jnp.int32, (2 * BLOCK, 1), 0) < BLOCK
    lo_half2 = lax.broadcasted_iota(jnp.int32, (2 * BLOCK, LANES), 1) < HEAD_DIM

    out_blocks = []
    for j in range(N_KV_HEADS):
        q2 = jnp.concatenate([qn[:, (2 * j) * LANES:(2 * j + 1) * LANES],
                              qn[:, (2 * j + 1) * LANES:(2 * j + 2) * LANES]], axis=0)
        s = _dot_nt(q2, kband_ref[j])
        s = jnp.minimum(s + bias_ref[tables * N_KV_HEADS + j], cap_ref[tables])
        es, sink_terms = [], []
        for slot in range(2):
            sc = s[:, slot * BAND:(slot + 1) * BAND]
            sink = jnp.where(top_rows, sinks_ref[GROUP * j + slot] * LOG2E,
                             sinks_ref[GROUP * j + HEADS_PER_LANE_BLOCK + slot] * LOG2E)
            m = jnp.max(sc, axis=-1, keepdims=True)
            es.append(jnp.exp2(sc - m).astype(BF16))
            sink_terms.append(jnp.exp2(sink - m))
        o = _dot(jnp.concatenate(es, axis=-1), vband_ref[j])
        den = o[:, LANES:] + jnp.where(lo_half2, sink_terms[0], sink_terms[1])
        outp = o[:, :LANES] * (1.0 / den)
        out_blocks += [outp[:BLOCK], outp[BLOCK:]]
    out_b = jnp.concatenate(out_blocks, axis=-1)
    o_ref[:, SGU_WIDTH:] = _rms_rows(out_b, gb_ref[...]).astype(BF16)


def _t5_bucket_table():
    qi = np.arange(BLOCK)[:, None]
    kj = np.arange(BAND)[None, :]
    n = np.maximum(qi + BLOCK - kj, 0)
    max_exact = NUM_BUCKETS // 2
    nf = np.maximum(n, 1).astype(np.float32)
    large = max_exact + (np.log(nf / np.float32(max_exact))
                         / np.float32(math.log(MAX_DISTANCE / max_exact))
                         * np.float32(NUM_BUCKETS - max_exact)).astype(np.int32)
    large = np.minimum(large, NUM_BUCKETS - 1)
    return np.where(n < max_exact, n, large).astype(np.int32)


def _head_segment_matrices():
    lane = np.arange(SGU_WIDTH)
    seg = (lane[:, None] // HEAD_DIM == np.arange(LANES)[None, :])
    return seg.astype(np.float32), seg.T.astype(np.float32)


def _cast_slabs(weights):
    steps = SEQ // BLOCK
    in_specs, out_specs, out_shapes, holds = [], [], [], []
    for w, layer in weights:
        _, rows, cols = w.shape
        hold = 1
        while (rows * hold) % steps or (rows * hold // steps) % BF16_SUBLANES:
            hold *= 2
        slab = rows * hold // steps
        in_specs.append(pl.BlockSpec((None, slab, cols),
                                     lambda i, layer=layer, hold=hold: (layer, i // hold, 0)))
        out_specs.append(pl.BlockSpec((slab, cols), lambda i, hold=hold: (i // hold, 0)))
        out_shapes.append(jax.ShapeDtypeStruct((rows, cols), BF16))
        holds.append(hold)
    return in_specs, out_specs, out_shapes, tuple(holds)


def _mixer(z, sinks, rel_bias, sgu_g, sgu_w, sgu_b, qg, kg, ga, gb, cast_weights):
    seg, exp = _head_segment_matrices()
    row = lambda n: pl.BlockSpec((1, n), lambda i: (0, 0))
    smem = pl.BlockSpec(memory_space=pltpu.SMEM)
    zcol = lambda w, c: pl.BlockSpec((BLOCK, w), lambda i: (i, c))
    kv_col = (2 * SGU_WIDTH + ATTN_WIDTH) // (2 * KV_WIDTH)
    cast_in, cast_out, cast_shapes, holds = _cast_slabs(cast_weights)
    cast_bytes = sum(2 * (4 + 2) * math.prod(s.block_shape[-2:]) for s in cast_in)
    return pl.pallas_call(
        functools.partial(_mixer_kernel, cast_holds=holds),
        out_shape=[jax.ShapeDtypeStruct((SEQ, D_MODEL), BF16)] + cast_shapes,
        grid=(SEQ // BLOCK,),
        in_specs=[
            smem, smem,
            zcol(SGU_WIDTH, 0), zcol(SGU_WIDTH, 1), zcol(ATTN_WIDTH, 2),
            zcol(2 * KV_WIDTH, kv_col),
            pl.BlockSpec((BLOCK, 2 * KV_WIDTH), lambda i: (jnp.maximum(i - 1, 0), kv_col)),
            row(SGU_WIDTH),
            pl.BlockSpec((N_SGU_HEADS, CHUNK, CHUNK), lambda i: (0, 0, 0)),
            pl.BlockSpec((CHUNK, SGU_WIDTH), lambda i: (0, 0)),
            row(ATTN_WIDTH), row(KV_WIDTH),
            pl.BlockSpec((BLOCK, BAND), lambda i: (0, 0)),
            pl.BlockSpec((SGU_WIDTH, LANES), lambda i: (0, 0)),
            pl.BlockSpec((LANES, SGU_WIDTH), lambda i: (0, 0)),
            row(SGU_WIDTH), row(ATTN_WIDTH),
        ] + cast_in,
        out_specs=[pl.BlockSpec((BLOCK, D_MODEL), lambda i: (i, 0))] + cast_out,
        scratch_shapes=[
            pltpu.VMEM((N_LANE_BLOCKS, 2 * CHUNK, CHUNK), BF16),
            pltpu.VMEM((2 * N_KV_HEADS, 2 * BLOCK, 2 * BAND), F32),
            pltpu.VMEM((2, 2 * BLOCK, 2 * BAND), F32),
            pltpu.VMEM((N_KV_HEADS, 2 * BAND, LANES), BF16),
            pltpu.VMEM((N_KV_HEADS, 2 * BAND, 2 * LANES), BF16),
            pltpu.VMEM((2 * BLOCK, SGU_WIDTH), BF16),
            pltpu.VMEM((4 * BLOCK, LANES), BF16),
        ],
        compiler_params=pltpu.CompilerParams(
            dimension_semantics=("arbitrary",),
            vmem_limit_bytes=MIXER_VMEM_BYTES + cast_bytes),
        name="mixer",
    )(sinks, rel_bias, z, z, z, z, z, sgu_g, sgu_w, sgu_b, qg, kg,
      jnp.asarray(_t5_bucket_table()), jnp.asarray(seg, BF16), jnp.asarray(exp, BF16),
      ga, gb, *[w for w, _ in cast_weights])


def _ffn_kernel(x_ref, mix_ref, wo_ref, g_ref, wg_ref, wu_ref, wd_ref, o_ref, h_ref):
    f = pl.program_id(1)

    @pl.when(f == 0)
    def _():
        x1 = x_ref[...] + _dot(mix_ref[...], wo_ref[...])
        o_ref[...] = x1
        h_ref[...] = _rms_rows(x1, g_ref[...]).astype(BF16)

    h = h_ref[...]
    gate = _dot(h, wg_ref[...])
    up = _dot(h, wu_ref[...])
    act = (gate * jax.nn.sigmoid(gate) * up).astype(BF16)
    o_ref[...] += _dot(act, wd_ref[...])


def _ffn(x, mixed, w_out, g, w_gate, w_up, w_down):
    vmem = (4 * FFN_TM * D_MODEL * 4 + 2 * FFN_TM * D_MODEL * 2
            + 2 * D_MODEL * D_MODEL * 2 + 6 * D_MODEL * FFN_TF * 2
            + FFN_TM * D_MODEL * 2 + 4 * FFN_TM * FFN_TF * 4 + 4 * MIB)
    return pl.pallas_call(
        _ffn_kernel,
        out_shape=jax.ShapeDtypeStruct((SEQ, D_MODEL), F32),
        grid=(SEQ // FFN_TM, D_FF // FFN_TF),
        in_specs=[
            pl.BlockSpec((FFN_TM, D_MODEL), lambda i, f: (i, 0)),
            pl.BlockSpec((FFN_TM, D_MODEL), lambda i, f: (i, 0)),
            pl.BlockSpec((D_MODEL, D_MODEL), lambda i, f: (0, 0)),
            pl.BlockSpec((1, D_MODEL), lambda i, f: (0, 0)),
            pl.BlockSpec((D_MODEL, FFN_TF), lambda i, f: (0, f)),
            pl.BlockSpec((D_MODEL, FFN_TF), lambda i, f: (0, f)),
            pl.BlockSpec((FFN_TF, D_MODEL), lambda i, f: (f, 0)),
        ],
        out_specs=pl.BlockSpec((FFN_TM, D_MODEL), lambda i, f: (i, 0)),
        scratch_shapes=[pltpu.VMEM((FFN_TM, D_MODEL), BF16)],
        compiler_params=pltpu.CompilerParams(
            dimension_semantics=("arbitrary", "arbitrary"),
            vmem_limit_bytes=vmem),
        name="ffn",
    )(x, mixed, w_out, g, w_gate, w_up, w_down)


def kernel(x, rel_bias, norm1_g, w_in, sgu_norm_g, sgu_w, sgu_b, q_norm_g, k_norm_g,
           sinks, out_norm_a, out_norm_b, w_out, norm2_g, w_gate, w_up, w_down):
    assert x.shape == (1, SEQ, D_MODEL)
    h = x.reshape(SEQ, D_MODEL)
    w_in_b = w_in[0].astype(BF16)
    for l in range(DEPTH):
        z = _in_proj(h, norm1_g[l].reshape(1, D_MODEL), w_in_b)
        cast_weights = [(w_out, l), (w_gate, l), (w_up, l), (w_down, l)]
        if l + 1 < DEPTH:
            cast_weights.append((w_in, l + 1))
        mixed, w_out_b, w_gate_b, w_up_b, w_down_b, *w_in_next = _mixer(
            z, sinks[l], rel_bias,
            sgu_norm_g[l].reshape(1, SGU_WIDTH), sgu_w[l],
            jnp.repeat(sgu_b[l].T, HEAD_DIM, axis=1),
            jnp.tile(q_norm_g[l], N_Q_HEADS).reshape(1, ATTN_WIDTH),
            jnp.tile(k_norm_g[l], N_KV_HEADS).reshape(1, KV_WIDTH),
            out_norm_a[l].reshape(1, SGU_WIDTH), out_norm_b[l].reshape(1, ATTN_WIDTH),
            cast_weights)
        if w_in_next:
            w_in_b, = w_in_next
        h = _ffn(h, mixed, w_out_b, norm2_g[l].reshape(1, D_MODEL),
                 w_gate_b, w_up_b, w_down_b)
    return h.reshape(1, SEQ, D_MODEL)
```

```python
import functools
import math

import numpy as np
import jax
import jax.numpy as jnp
from jax import lax
from jax.experimental import pallas as pl
from jax.experimental.pallas import tpu as pltpu

D_MODEL = 2048
SEQ = 8192
DEPTH = 4
HEAD_DIM = 64
SGU_WIDTH = D_MODEL // 2
N_SGU_HEADS = SGU_WIDTH // HEAD_DIM
CHUNK = 128
ATTN_WIDTH = D_MODEL - SGU_WIDTH
N_Q_HEADS = ATTN_WIDTH // HEAD_DIM
N_KV_HEADS = 4
GROUP = N_Q_HEADS // N_KV_HEADS
KV_WIDTH = N_KV_HEADS * HEAD_DIM
WINDOW = 128
BLOCK = 128
BAND = 2 * BLOCK
NUM_BUCKETS = 32
MAX_DISTANCE = 128
IN_WIDTH = 2 * SGU_WIDTH + ATTN_WIDTH + 2 * KV_WIDTH
D_FF = -(-8 * D_MODEL // (3 * 256)) * 256
EPS = 1e-6
NEG_INF = -1e30
SCALE = 1.0 / math.sqrt(HEAD_DIM)
LOG2E = math.log2(math.e)
F32_MAX = float(np.finfo(np.float32).max)

LANES = 128
BF16_SUBLANES = 16
HEADS_PER_LANE_BLOCK = LANES // HEAD_DIM
N_LANE_BLOCKS = SGU_WIDTH // LANES
MIB = 1024 * 1024

F32 = jnp.float32
BF16 = jnp.bfloat16

IN_TM = 512
IN_TN = 512
N_GELU_TILES = (2 * SGU_WIDTH) // IN_TN
FFN_TM = 512
FFN_TF = 512
MIXER_NB = 2
MIXER_TB = MIXER_NB * BLOCK
MIXER_VMEM_BYTES = 24 * MIB

assert CHUNK == BLOCK == LANES and HEADS_PER_LANE_BLOCK == 2 and GROUP == 4


def _rms_rows(xf, g):
    ms = jnp.mean(xf * xf, axis=-1, keepdims=True)
    return xf * lax.rsqrt(ms + EPS) * g


def _split2(x):
    hi = x.astype(BF16)
    return hi, (x - hi.astype(F32)).astype(BF16)


def _dot(a, b):
    return jnp.dot(a, b, preferred_element_type=F32)


def _dot_nt(a, b):
    return lax.dot_general(a, b, (((1,), (1,)), ((), ())), preferred_element_type=F32)


def _in_proj_kernel(x_ref, g_ref, w_ref, o_ref):
    h = _rms_rows(x_ref[...], g_ref[...]).astype(BF16)
    for c in range(IN_WIDTH // IN_TN):
        cols = slice(c * IN_TN, (c + 1) * IN_TN)
        z = _dot(h, w_ref[:, cols])
        if c < N_GELU_TILES:
            z = 0.5 * z * (1.0 + lax.erf(z * (1.0 / math.sqrt(2.0))))
        o_ref[:, cols] = z.astype(BF16)


def _in_proj(x, g, w):
    vmem = (2 * IN_TM * D_MODEL * 4 + 2 * D_MODEL * IN_WIDTH * 2 + 2 * IN_TM * IN_WIDTH * 2
            + IN_TM * D_MODEL * 2 + 4 * IN_TM * IN_TN * 4 + 4 * MIB)
    return pl.pallas_call(
        _in_proj_kernel,
        out_shape=jax.ShapeDtypeStruct((SEQ, IN_WIDTH), BF16),
        grid=(SEQ // IN_TM,),
        in_specs=[
            pl.BlockSpec((IN_TM, D_MODEL), lambda i: (i, 0)),
            pl.BlockSpec((1, D_MODEL), lambda i: (0, 0)),
            pl.BlockSpec((D_MODEL, IN_WIDTH), lambda i: (0, 0)),
        ],
        out_specs=pl.BlockSpec((IN_TM, IN_WIDTH), lambda i: (i, 0)),
        compiler_params=pltpu.CompilerParams(
            dimension_semantics=("arbitrary",),
            vmem_limit_bytes=vmem),
        name="in_proj",
    )(x, g, w)


def _mixer_kernel(sinks_ref, relb_ref,
                  zu_ref, zv_ref, zq_ref, zkv_ref, zkvp_ref,
                  sgu_g_ref, sgu_w_ref, sgu_b_ref, qg_ref, kg_ref,
                  bucket_ref, seg_ref, exp_ref, ga_ref, gb_ref,
                  *rest, cast_holds):
    n_cast = len(cast_holds)
    cast_in = rest[:n_cast]
    o_ref = rest[n_cast]
    cast_out = rest[n_cast + 1:2 * n_cast + 1]
    wm_ref, bias_ref, cap_ref, kband_ref, vband_ref, sq_ref, rr_ref = rest[2 * n_cast + 1:]
    i = pl.program_id(0)

    for src, dst, hold in zip(cast_in, cast_out, cast_holds):
        if hold == 1:
            dst[...] = src[...].astype(BF16)
        else:
            @pl.when(i % hold == 0)
            def _(src=src, dst=dst):
                dst[...] = src[...].astype(BF16)

    lane = lax.broadcasted_iota(jnp.int32, (BLOCK, LANES), 1)
    lo_half = lane < HEAD_DIM

    @pl.when(i == 0)
    def _():
        t_idx = lax.broadcasted_iota(jnp.int32, (CHUNK, CHUNK), 0)
        s_idx = lax.broadcasted_iota(jnp.int32, (CHUNK, CHUNK), 1)
        keep = s_idx <= t_idx

        def mask_pair(p, c):
            wm_ref[p, :CHUNK, :] = jnp.where(keep, sgu_w_ref[2 * p], 0.0).astype(BF16)
            wm_ref[p, CHUNK:, :] = jnp.where(keep, sgu_w_ref[2 * p + 1], 0.0).astype(BF16)
            return c
        lax.fori_loop(0, N_LANE_BLOCKS, mask_pair, 0)

        bucket = bucket_ref[...]
        qi = lax.broadcasted_iota(jnp.int32, (BLOCK, BAND), 0)
        kj = lax.broadcasted_iota(jnp.int32, (BLOCK, BAND), 1)
        dist = qi + BLOCK - kj
        in_window = (dist >= 0) & (dist < WINDOW)
        oks = (in_window, in_window & (kj >= BLOCK))
        for rb in range(2):
            rows = slice(rb * BLOCK, (rb + 1) * BLOCK)
            for slot in range(2):
                cols = slice(slot * BAND, (slot + 1) * BAND)
                for first, ok in enumerate(oks):
                    cap_ref[first, rows, cols] = jnp.where(ok, F32_MAX, NEG_INF)

                def bias_head(j, c, rows=rows, cols=cols, rb=rb, slot=slot):
                    h = GROUP * j + HEADS_PER_LANE_BLOCK * rb + slot
                    acc = jnp.zeros((BLOCK, BAND), F32)
                    for b in range(NUM_BUCKETS):
                        acc = jnp.where(bucket == b, relb_ref[b, h] * LOG2E, acc)
                    for first, ok in enumerate(oks):
                        bias_ref[first * N_KV_HEADS + j, rows, cols] = jnp.where(ok, acc, 0.0)
                    return c
                lax.fori_loop(0, N_KV_HEADS, bias_head, 0)

        row = lax.broadcasted_iota(jnp.int32, (2 * BAND, 2 * LANES), 0)
        col = lax.broadcasted_iota(jnp.int32, (2 * BAND, 2 * LANES), 1)
        ones = (((row < BAND) & (col >= LANES) & (col < LANES + HEAD_DIM))
                | ((row >= BAND) & (col >= LANES + HEAD_DIM)))
        pattern = jnp.where(ones, 1.0, 0.0).astype(BF16)
        for sb in range(MIXER_NB):
            for j in range(N_KV_HEADS):
                vband_ref[sb, j] = pattern

    lo_band = lax.broadcasted_iota(jnp.int32, (BAND, LANES), 1) < HEAD_DIM
    top_rows = lax.broadcasted_iota(jnp.int32, (2 * BLOCK, 1), 0) < BLOCK
    lo_half2 = lax.broadcasted_iota(jnp.int32, (2 * BLOCK, LANES), 1) < HEAD_DIM
    segk = seg_ref[:KV_WIDTH, :]
    expk = exp_ref[:, :KV_WIDTH]

    for sb in range(MIXER_NB):
        blk = slice(sb * BLOCK, (sb + 1) * BLOCK)
        kband, vband, sq, rr = kband_ref.at[sb], vband_ref.at[sb], sq_ref.at[sb], rr_ref.at[sb]

        vb = zv_ref[blk, :]
        qb = zq_ref[blk, :]
        prev = zkvp_ref[...] if sb == 0 else zkv_ref[(sb - 1) * BLOCK:sb * BLOCK, :]
        kvb = jnp.concatenate([prev, zkv_ref[blk, :]], axis=0)
        kb = kvb[:, :KV_WIDTH]
        vv = kvb[:, KV_WIDTH:].astype(F32)

        sq[:BLOCK] = vb * vb
        sq[BLOCK:] = qb * qb
        ss = _dot(sq[...], seg_ref[...])
        rh, rl = _split2(lax.rsqrt(ss * (1.0 / HEAD_DIM) + EPS))
        rr[:2 * BLOCK] = rh
        rr[2 * BLOCK:] = rl
        rb = _dot(rr[...], exp_ref[...])
        rb = rb[:2 * BLOCK] + rb[2 * BLOCK:]
        vn = (vb.astype(F32) * rb[:BLOCK] * sgu_g_ref[...]).astype(BF16)
        qn = (qb.astype(F32) * rb[BLOCK:] * (qg_ref[...] * (SCALE * LOG2E))).astype(BF16)

        rkh, rkl = _split2(lax.rsqrt(_dot(kb * kb, segk) * (1.0 / HEAD_DIM) + EPS))
        kn = kb.astype(F32) * (_dot(rkh, expk) + _dot(rkl, expk)) * kg_ref[...]

        gates = []
        for p in range(N_LANE_BLOCKS):
            g2 = _dot(wm_ref[p], vn[:, p * LANES:(p + 1) * LANES])
            gates.append(jnp.where(lo_half, g2[:CHUNK], g2[CHUNK:]))
        gate = jnp.concatenate(gates, axis=-1) + sgu_b_ref[...]
        out_a = zu_ref[blk, :].astype(F32) * gate
        o_ref[blk, :SGU_WIDTH] = _rms_rows(out_a, ga_ref[...]).astype(BF16)

        for j in range(N_KV_HEADS):
            p, half = divmod(j, HEADS_PER_LANE_BLOCK)
            keep = lo_band if half == 0 else jnp.logical_not(lo_band)
            k_same = jnp.where(keep, kn[:, p * LANES:(p + 1) * LANES], 0.0)
            v_same = jnp.where(keep, vv[:, p * LANES:(p + 1) * LANES], 0.0)
            k_other = pltpu.roll(k_same, HEAD_DIM, 1)
            v_other = pltpu.roll(v_same, HEAD_DIM, 1)
            k_slots = (k_same, k_other) if half == 0 else (k_other, k_same)
            v_slots = (v_same, v_other) if half == 0 else (v_other, v_same)
            for slot in range(2):
                rows = slice(slot * BAND, (slot + 1) * BAND)
                kband[j, rows, :] = k_slots[slot].astype(BF16)
                vband[j, rows, :LANES] = v_slots[slot].astype(BF16)

        tables = jnp.where(i == 0, 1, 0) if sb == 0 else 0

        out_blocks = []
        for j in range(N_KV_HEADS):
            q2 = jnp.concatenate([qn[:, (2 * j) * LANES:(2 * j + 1) * LANES],
                                  qn[:, (2 * j + 1) * LANES:(2 * j + 2) * LANES]], axis=0)
            s = _dot_nt(q2, kband[j])
            s = jnp.minimum(s + bias_ref[tables * N_KV_HEADS + j], cap_ref[tables])
            es, sink_terms = [], []
            for slot in range(2):
                sc = s[:, slot * BAND:(slot + 1) * BAND]
                sink = jnp.where(top_rows, sinks_ref[GROUP * j + slot] * LOG2E,
                                 sinks_ref[GROUP * j + HEADS_PER_LANE_BLOCK + slot] * LOG2E)
                m = jnp.max(sc, axis=-1, keepdims=True)
                es.append(jnp.exp2(sc - m).astype(BF16))
                sink_terms.append(jnp.exp2(sink - m))
            o = _dot(jnp.concatenate(es, axis=-1), vband[j])
            den = o[:, LANES:] + jnp.where(lo_half2, sink_terms[0], sink_terms[1])
            outp = o[:, :LANES] * (1.0 / den)
            out_blocks += [outp[:BLOCK], outp[BLOCK:]]
        out_b = jnp.concatenate(out_blocks, axis=-1)
        o_ref[blk, SGU_WIDTH:] = _rms_rows(out_b, gb_ref[...]).astype(BF16)


def _t5_bucket_table():
    qi = np.arange(BLOCK)[:, None]
    kj = np.arange(BAND)[None, :]
    n = np.maximum(qi + BLOCK - kj, 0)
    max_exact = NUM_BUCKETS // 2
    nf = np.maximum(n, 1).astype(np.float32)
    large = max_exact + (np.log(nf / np.float32(max_exact))
                         / np.float32(math.log(MAX_DISTANCE / max_exact))
                         * np.float32(NUM_BUCKETS - max_exact)).astype(np.int32)
    large = np.minimum(large, NUM_BUCKETS - 1)
    return np.where(n < max_exact, n, large).astype(np.int32)


def _head_segment_matrices():
    lane = np.arange(SGU_WIDTH)
    seg = (lane[:, None] // HEAD_DIM == np.arange(LANES)[None, :])
    return seg.astype(np.float32), seg.T.astype(np.float32)


def _cast_slabs(weights):
    steps = SEQ // MIXER_TB
    in_specs, out_specs, out_shapes, holds = [], [], [], []
    for w, layer in weights:
        _, rows, cols = w.shape
        hold = 1
        while (rows * hold) % steps or (rows * hold // steps) % BF16_SUBLANES:
            hold *= 2
        slab = rows * hold // steps
        in_specs.append(pl.BlockSpec((None, slab, cols),
                                     lambda i, layer=layer, hold=hold: (layer, i // hold, 0)))
        out_specs.append(pl.BlockSpec((slab, cols), lambda i, hold=hold: (i // hold, 0)))
        out_shapes.append(jax.ShapeDtypeStruct((rows, cols), BF16))
        holds.append(hold)
    return in_specs, out_specs, out_shapes, tuple(holds)


def _mixer(z, sinks, rel_bias, sgu_g, sgu_w, sgu_b, qg, kg, ga, gb, cast_weights):
    seg, exp = _head_segment_matrices()
    row = lambda n: pl.BlockSpec((1, n), lambda i: (0, 0))
    smem = pl.BlockSpec(memory_space=pltpu.SMEM)
    zcol = lambda w, c: pl.BlockSpec((MIXER_TB, w), lambda i: (i, c))
    kv_col = (2 * SGU_WIDTH + ATTN_WIDTH) // (2 * KV_WIDTH)
    cast_in, cast_out, cast_shapes, holds = _cast_slabs(cast_weights)
    cast_bytes = sum(2 * (4 + 2) * math.prod(s.block_shape[-2:]) for s in cast_in)
    return pl.pallas_call(
        functools.partial(_mixer_kernel, cast_holds=holds),
        out_shape=[jax.ShapeDtypeStruct((SEQ, D_MODEL), BF16)] + cast_shapes,
        grid=(SEQ // MIXER_TB,),
        in_specs=[
            smem, smem,
            zcol(SGU_WIDTH, 0), zcol(SGU_WIDTH, 1), zcol(ATTN_WIDTH, 2),
            zcol(2 * KV_WIDTH, kv_col),
            pl.BlockSpec((BLOCK, 2 * KV_WIDTH),
                         lambda i: (jnp.maximum(MIXER_NB * i - 1, 0), kv_col)),
            row(SGU_WIDTH),
            pl.BlockSpec((N_SGU_HEADS, CHUNK, CHUNK), lambda i: (0, 0, 0)),
            pl.BlockSpec((CHUNK, SGU_WIDTH), lambda i: (0, 0)),
            row(ATTN_WIDTH), row(KV_WIDTH),
            pl.BlockSpec((BLOCK, BAND), lambda i: (0, 0)),
            pl.BlockSpec((SGU_WIDTH, LANES), lambda i: (0, 0)),
            pl.BlockSpec((LANES, SGU_WIDTH), lambda i: (0, 0)),
            row(SGU_WIDTH), row(ATTN_WIDTH),
        ] + cast_in,
        out_specs=[pl.BlockSpec((MIXER_TB, D_MODEL), lambda i: (i, 0))] + cast_out,
        scratch_shapes=[
            pltpu.VMEM((N_LANE_BLOCKS, 2 * CHUNK, CHUNK), BF16),
            pltpu.VMEM((2 * N_KV_HEADS, 2 * BLOCK, 2 * BAND), F32),
            pltpu.VMEM((2, 2 * BLOCK, 2 * BAND), F32),
            pltpu.VMEM((MIXER_NB, N_KV_HEADS, 2 * BAND, LANES), BF16),
            pltpu.VMEM((MIXER_NB, N_KV_HEADS, 2 * BAND, 2 * LANES), BF16),
            pltpu.VMEM((MIXER_NB, 2 * BLOCK, SGU_WIDTH), BF16),
            pltpu.VMEM((MIXER_NB, 4 * BLOCK, LANES), BF16),
        ],
        compiler_params=pltpu.CompilerParams(
            dimension_semantics=("arbitrary",),
            vmem_limit_bytes=MIXER_VMEM_BYTES + cast_bytes),
        name="mixer",
    )(sinks, rel_bias, z, z, z, z, z, sgu_g, sgu_w, sgu_b, qg, kg,
      jnp.asarray(_t5_bucket_table()), jnp.asarray(seg, BF16), jnp.asarray(exp, BF16),
      ga, gb, *[w for w, _ in cast_weights])


def _ffn_kernel(x_ref, mix_ref, wo_ref, g_ref, wg_ref, wu_ref, wd_ref, o_ref, h_ref):
    f = pl.program_id(1)

    @pl.when(f == 0)
    def _():
        x1 = x_ref[...] + _dot(mix_ref[...], wo_ref[...])
        o_ref[...] = x1
        h_ref[...] = _rms_rows(x1, g_ref[...]).astype(BF16)

    h = h_ref[...]
    gate = _dot(h, wg_ref[...])
    up = _dot(h, wu_ref[...])
    act = (gate * jax.nn.sigmoid(gate) * up).astype(BF16)
    o_ref[...] += _dot(act, wd_ref[...])


def _ffn(x, mixed, w_out, g, w_gate, w_up, w_down):
    vmem = (4 * FFN_TM * D_MODEL * 4 + 2 * FFN_TM * D_MODEL * 2
            + 2 * D_MODEL * D_MODEL * 2 + 6 * D_MODEL * FFN_TF * 2
            + FFN_TM * D_MODEL * 2 + 4 * FFN_TM * FFN_TF * 4 + 4 * MIB)
    return pl.pallas_call(
        _ffn_kernel,
        out_shape=jax.ShapeDtypeStruct((SEQ, D_MODEL), F32),
        grid=(SEQ // FFN_TM, D_FF // FFN_TF),
        in_specs=[
            pl.BlockSpec((FFN_TM, D_MODEL), lambda i, f: (i, 0)),
            pl.BlockSpec((FFN_TM, D_MODEL), lambda i, f: (i, 0)),
            pl.BlockSpec((D_MODEL, D_MODEL), lambda i, f: (0, 0)),
            pl.BlockSpec((1, D_MODEL), lambda i, f: (0, 0)),
            pl.BlockSpec((D_MODEL, FFN_TF), lambda i, f: (0, f)),
            pl.BlockSpec((D_MODEL, FFN_TF), lambda i, f: (0, f)),
            pl.BlockSpec((FFN_TF, D_MODEL), lambda i, f: (f, 0)),
        ],
        out_specs=pl.BlockSpec((FFN_TM, D_MODEL), lambda i, f: (i, 0)),
        scratch_shapes=[pltpu.VMEM((FFN_TM, D_MODEL), BF16)],
        compiler_params=pltpu.CompilerParams(
            dimension_semantics=("arbitrary", "arbitrary"),
            vmem_limit_bytes=vmem),
        name="ffn",
    )(x, mixed, w_out, g, w_gate, w_up, w_down)


def kernel(x, rel_bias, norm1_g, w_in, sgu_norm_g, sgu_w, sgu_b, q_norm_g, k_norm_g,
           sinks, out_norm_a, out_norm_b, w_out, norm2_g, w_gate, w_up, w_down):
    assert x.shape == (1, SEQ, D_MODEL)
    h = x.reshape(SEQ, D_MODEL)
    w_in_b = w_in[0].astype(BF16)
    for l in range(DEPTH):
        z = _in_proj(h, norm1_g[l].reshape(1, D_MODEL), w_in_b)
        cast_weights = [(w_out, l), (w_gate, l), (w_up, l), (w_down, l)]
        if l + 1 < DEPTH:
            cast_weights.append((w_in, l + 1))
        mixed, w_out_b, w_gate_b, w_up_b, w_down_b, *w_in_next = _mixer(
            z, sinks[l], rel_bias,
            sgu_norm_g[l].reshape(1, SGU_WIDTH), sgu_w[l],
            jnp.repeat(sgu_b[l].T, HEAD_DIM, axis=1),
            jnp.tile(q_norm_g[l], N_Q_HEADS).reshape(1, ATTN_WIDTH),
            jnp.tile(k_norm_g[l], N_KV_HEADS).reshape(1, KV_WIDTH),
            out_norm_a[l].reshape(1, SGU_WIDTH), out_norm_b[l].reshape(1, ATTN_WIDTH),
            cast_weights)
        if w_in_next:
            w_in_b, = w_in_next
        h = _ffn(h, mixed, w_out_b, norm2_g[l].reshape(1, D_MODEL),
                 w_gate_b, w_up_b, w_down_b)
    return h.reshape(1, SEQ, D_MODEL)
```

```python
import functools
import math

import numpy as np
import jax
import jax.numpy as jnp
from jax import lax
from jax.experimental import pallas as pl
from jax.experimental.pallas import tpu as pltpu

D_MODEL = 2048
SEQ = 8192
DEPTH = 4
HEAD_DIM = 64
SGU_WIDTH = D_MODEL // 2
N_SGU_HEADS = SGU_WIDTH // HEAD_DIM
CHUNK = 128
ATTN_WIDTH = D_MODEL - SGU_WIDTH
N_Q_HEADS = ATTN_WIDTH // HEAD_DIM
N_KV_HEADS = 4
GROUP = N_Q_HEADS // N_KV_HEADS
KV_WIDTH = N_KV_HEADS * HEAD_DIM
WINDOW = 128
BLOCK = 128
BAND = 2 * BLOCK
NUM_BUCKETS = 32
MAX_DISTANCE = 128
IN_WIDTH = 2 * SGU_WIDTH + ATTN_WIDTH + 2 * KV_WIDTH
D_FF = -(-8 * D_MODEL // (3 * 256)) * 256
EPS = 1e-6
NEG_INF = -1e30
SCALE = 1.0 / math.sqrt(HEAD_DIM)
LOG2E = math.log2(math.e)
F32_MAX = float(np.finfo(np.float32).max)

LANES = 128
BF16_SUBLANES = 16
HEADS_PER_LANE_BLOCK = LANES // HEAD_DIM
N_LANE_BLOCKS = SGU_WIDTH // LANES
MIB = 1024 * 1024

F32 = jnp.float32
BF16 = jnp.bfloat16

IN_TM = 1024
IN_TN = 512
N_GELU_TILES = (2 * SGU_WIDTH) // IN_TN
FFN_TM = 512
FFN_TF = 512
MIXER_NB = 2
MIXER_TB = MIXER_NB * BLOCK
MIXER_VMEM_BYTES = 24 * MIB

assert CHUNK == BLOCK == LANES and HEADS_PER_LANE_BLOCK == 2 and GROUP == 4


def _rms_rows(xf, g):
    ms = jnp.mean(xf * xf, axis=-1, keepdims=True)
    return xf * lax.rsqrt(ms + EPS) * g


def _split2(x):
    hi = x.astype(BF16)
    return hi, (x - hi.astype(F32)).astype(BF16)


def _dot(a, b):
    return jnp.dot(a, b, preferred_element_type=F32)


def _dot_nt(a, b):
    return lax.dot_general(a, b, (((1,), (1,)), ((), ())), preferred_element_type=F32)


def _in_proj_kernel(x_ref, g_ref, w_ref, o_ref):
    h = _rms_rows(x_ref[...], g_ref[...]).astype(BF16)
    for c in range(IN_WIDTH // IN_TN):
        cols = slice(c * IN_TN, (c + 1) * IN_TN)
        z = _dot(h, w_ref[:, cols].astype(BF16))
        if c < N_GELU_TILES:
            z = 0.5 * z * (1.0 + lax.erf(z * (1.0 / math.sqrt(2.0))))
        o_ref[:, cols] = z.astype(BF16)


def _in_proj(x, g, w, layer=None):
    w_spec = (pl.BlockSpec((D_MODEL, IN_WIDTH), lambda i: (0, 0)) if layer is None else
              pl.BlockSpec((None, D_MODEL, IN_WIDTH), lambda i: (layer, 0, 0)))
    tm = IN_TM if layer is None else IN_TM // 2
    vmem = (2 * tm * D_MODEL * 4 + D_MODEL * IN_WIDTH * w.dtype.itemsize
            + 2 * tm * IN_WIDTH * 2 + tm * D_MODEL * 2 + 4 * tm * IN_TN * 4 + 6 * MIB)
    return pl.pallas_call(
        _in_proj_kernel,
        out_shape=jax.ShapeDtypeStruct((SEQ, IN_WIDTH), BF16),
        grid=(SEQ // tm,),
        in_specs=[
            pl.BlockSpec((tm, D_MODEL), lambda i: (i, 0)),
            pl.BlockSpec((1, D_MODEL), lambda i: (0, 0)),
            w_spec,
        ],
        out_specs=pl.BlockSpec((tm, IN_WIDTH), lambda i: (i, 0)),
        compiler_params=pltpu.CompilerParams(
            dimension_semantics=("arbitrary",),
            vmem_limit_bytes=vmem),
        name="in_proj",
    )(x, g, w)


def _mixer_kernel(sinks_ref, relb_ref,
                  zu_ref, zv_ref, zq_ref, zkv_ref, zkvp_ref,
                  sgu_g_ref, sgu_w_ref, sgu_b_ref, qg_ref, kg_ref,
                  bucket_ref, seg_ref, exp_ref, ga_ref, gb_ref,
                  *rest, cast_holds):
    n_cast = len(cast_holds)
    cast_in = rest[:n_cast]
    o_ref = rest[n_cast]
    cast_out = rest[n_cast + 1:2 * n_cast + 1]
    wm_ref, bias_ref, cap_ref, kband_ref, vband_ref, sq_ref, rr_ref = rest[2 * n_cast + 1:]
    i = pl.program_id(0)

    for src, dst, hold in zip(cast_in, cast_out, cast_holds):
        if hold == 1:
            dst[...] = src[...].astype(BF16)
        else:
            @pl.when(i % hold == 0)
            def _(src=src, dst=dst):
                dst[...] = src[...].astype(BF16)

    lane = lax.broadcasted_iota(jnp.int32, (BLOCK, LANES), 1)
    lo_half = lane < HEAD_DIM

    @pl.when(i == 0)
    def _():
        t_idx = lax.broadcasted_iota(jnp.int32, (CHUNK, CHUNK), 0)
        s_idx = lax.broadcasted_iota(jnp.int32, (CHUNK, CHUNK), 1)
        keep = s_idx <= t_idx

        def mask_pair(p, c):
            wm_ref[p, :CHUNK, :] = jnp.where(keep, sgu_w_ref[2 * p], 0.0).astype(BF16)
            wm_ref[p, CHUNK:, :] = jnp.where(keep, sgu_w_ref[2 * p + 1], 0.0).astype(BF16)
            return c
        lax.fori_loop(0, N_LANE_BLOCKS, mask_pair, 0)

        bucket = bucket_ref[...]
        qi = lax.broadcasted_iota(jnp.int32, (BLOCK, BAND), 0)
        kj = lax.broadcasted_iota(jnp.int32, (BLOCK, BAND), 1)
        dist = qi + BLOCK - kj
        in_window = (dist >= 0) & (dist < WINDOW)
        oks = (in_window, in_window & (kj >= BLOCK))
        for rb in range(2):
            rows = slice(rb * BLOCK, (rb + 1) * BLOCK)
            for slot in range(2):
                cols = slice(slot * BAND, (slot + 1) * BAND)
                for first, ok in enumerate(oks):
                    cap_ref[first, rows, cols] = jnp.where(ok, F32_MAX, NEG_INF)

                def bias_head(j, c, rows=rows, cols=cols, rb=rb, slot=slot):
                    h = GROUP * j + HEADS_PER_LANE_BLOCK * rb + slot
                    acc = jnp.zeros((BLOCK, BAND), F32)
                    for b in range(NUM_BUCKETS):
                        acc = jnp.where(bucket == b, relb_ref[b, h] * LOG2E, acc)
                    for first, ok in enumerate(oks):
                        bias_ref[first * N_KV_HEADS + j, rows, cols] = jnp.where(ok, acc, 0.0)
                    return c
                lax.fori_loop(0, N_KV_HEADS, bias_head, 0)

        row = lax.broadcasted_iota(jnp.int32, (2 * BAND, 2 * LANES), 0)
        col = lax.broadcasted_iota(jnp.int32, (2 * BAND, 2 * LANES), 1)
        ones = (((row < BAND) & (col >= LANES) & (col < LANES + HEAD_DIM))
                | ((row >= BAND) & (col >= LANES + HEAD_DIM)))
        pattern = jnp.where(ones, 1.0, 0.0).astype(BF16)
        for sb in range(MIXER_NB):
            for j in range(N_KV_HEADS):
                vband_ref[sb, j] = pattern

    lo_band = lax.broadcasted_iota(jnp.int32, (BAND, LANES), 1) < HEAD_DIM
    top_rows = lax.broadcasted_iota(jnp.int32, (2 * BLOCK, 1), 0) < BLOCK
    lo_half2 = lax.broadcasted_iota(jnp.int32, (2 * BLOCK, LANES), 1) < HEAD_DIM
    segk = seg_ref[:KV_WIDTH, :]
    expk = exp_ref[:, :KV_WIDTH]

    for sb in range(MIXER_NB):
        blk = slice(sb * BLOCK, (sb + 1) * BLOCK)
        kband, vband, sq, rr = kband_ref.at[sb], vband_ref.at[sb], sq_ref.at[sb], rr_ref.at[sb]

        vb = zv_ref[blk, :]
        qb = zq_ref[blk, :]
        prev = zkvp_ref[...] if sb == 0 else zkv_ref[(sb - 1) * BLOCK:sb * BLOCK, :]
        kvb = jnp.concatenate([prev, zkv_ref[blk, :]], axis=0)
        kb = kvb[:, :KV_WIDTH]
        vv = kvb[:, KV_WIDTH:].astype(F32)

        sq[:BLOCK] = vb * vb
        sq[BLOCK:] = qb * qb
        ss = _dot(sq[...], seg_ref[...])
        rh, rl = _split2(lax.rsqrt(ss * (1.0 / HEAD_DIM) + EPS))
        rr[:2 * BLOCK] = rh
        rr[2 * BLOCK:] = rl
        rb = _dot(rr[...], exp_ref[...])
        rb = rb[:2 * BLOCK] + rb[2 * BLOCK:]
        vn = (vb.astype(F32) * rb[:BLOCK] * sgu_g_ref[...]).astype(BF16)
        qn = (qb.astype(F32) * rb[BLOCK:] * (qg_ref[...] * (SCALE * LOG2E))).astype(BF16)

        rkh, rkl = _split2(lax.rsqrt(_dot(kb * kb, segk) * (1.0 / HEAD_DIM) + EPS))
        kn = kb.astype(F32) * (_dot(rkh, expk) + _dot(rkl, expk)) * kg_ref[...]

        gates = []
        for p in range(N_LANE_BLOCKS):
            g2 = _dot(wm_ref[p], vn[:, p * LANES:(p + 1) * LANES])
            gates.append(jnp.where(lo_half, g2[:CHUNK], g2[CHUNK:]))
        gate = jnp.concatenate(gates, axis=-1) + sgu_b_ref[...]
        out_a = zu_ref[blk, :].astype(F32) * gate
        o_ref[blk, :SGU_WIDTH] = _rms_rows(out_a, ga_ref[...]).astype(BF16)

        for j in range(N_KV_HEADS):
            p, half = divmod(j, HEADS_PER_LANE_BLOCK)
            keep = lo_band if half == 0 else jnp.logical_not(lo_band)
            k_same = jnp.where(keep, kn[:, p * LANES:(p + 1) * LANES], 0.0)
            v_same = jnp.where(keep, vv[:, p * LANES:(p + 1) * LANES], 0.0)
            k_other = pltpu.roll(k_same, HEAD_DIM, 1)
            v_other = pltpu.roll(v_same, HEAD_DIM, 1)
            k_slots = (k_same, k_other) if half == 0 else (k_other, k_same)
            v_slots = (v_same, v_other) if half == 0 else (v_other, v_same)
            for slot in range(2):
                rows = slice(slot * BAND, (slot + 1) * BAND)
                kband[j, rows, :] = k_slots[slot].astype(BF16)
                vband[j, rows, :LANES] = v_slots[slot].astype(BF16)

        tables = jnp.where(i == 0, 1, 0) if sb == 0 else 0

        out_blocks = []
        for j in range(N_KV_HEADS):
            q2 = jnp.concatenate([qn[:, (2 * j) * LANES:(2 * j + 1) * LANES],
                                  qn[:, (2 * j + 1) * LANES:(2 * j + 2) * LANES]], axis=0)
            s = _dot_nt(q2, kband[j])
            s = jnp.minimum(s + bias_ref[tables * N_KV_HEADS + j], cap_ref[tables])
            es, sink_terms = [], []
            for slot in range(2):
                sc = s[:, slot * BAND:(slot + 1) * BAND]
                sink = jnp.where(top_rows, sinks_ref[GROUP * j + slot] * LOG2E,
                                 sinks_ref[GROUP * j + HEADS_PER_LANE_BLOCK + slot] * LOG2E)
                m = jnp.max(sc, axis=-1, keepdims=True)
                es.append(jnp.exp2(sc - m).astype(BF16))
                sink_terms.append(jnp.exp2(sink - m))
            o = _dot(jnp.concatenate(es, axis=-1), vband[j])
            den = o[:, LANES:] + jnp.where(lo_half2, sink_terms[0], sink_terms[1])
            outp = o[:, :LANES] * (1.0 / den)
            out_blocks += [outp[:BLOCK], outp[BLOCK:]]
        out_b = jnp.concatenate(out_blocks, axis=-1)
        o_ref[blk, SGU_WIDTH:] = _rms_rows(out_b, gb_ref[...]).astype(BF16)


def _t5_bucket_table():
    qi = np.arange(BLOCK)[:, None]
    kj = np.arange(BAND)[None, :]
    n = np.maximum(qi + BLOCK - kj, 0)
    max_exact = NUM_BUCKETS // 2
    nf = np.maximum(n, 1).astype(np.float32)
    large = max_exact + (np.log(nf / np.float32(max_exact))
                         / np.float32(math.log(MAX_DISTANCE / max_exact))
                         * np.float32(NUM_BUCKETS - max_exact)).astype(np.int32)
    large = np.minimum(large, NUM_BUCKETS - 1)
    return np.where(n < max_exact, n, large).astype(np.int32)


def _head_segment_matrices():
    lane = np.arange(SGU_WIDTH)
    seg = (lane[:, None] // HEAD_DIM == np.arange(LANES)[None, :])
    return seg.astype(np.float32), seg.T.astype(np.float32)


def _cast_slabs(weights):
    steps = SEQ // MIXER_TB
    in_specs, out_specs, out_shapes, holds = [], [], [], []
    for w, layer in weights:
        _, rows, cols = w.shape
        hold = 1
        while (rows * hold) % steps or (rows * hold // steps) % BF16_SUBLANES:
            hold *= 2
        slab = rows * hold // steps
        in_specs.append(pl.BlockSpec((None, slab, cols),
                                     lambda i, layer=layer, hold=hold: (layer, i // hold, 0)))
        out_specs.append(pl.BlockSpec((slab, cols), lambda i, hold=hold: (i // hold, 0)))
        out_shapes.append(jax.ShapeDtypeStruct((rows, cols), BF16))
        holds.append(hold)
    return in_specs, out_specs, out_shapes, tuple(holds)


def _mixer(z, sinks, rel_bias, sgu_g, sgu_w, layer, sgu_b, qg, kg, ga, gb, cast_weights):
    seg, exp = _head_segment_matrices()
    row = lambda n: pl.BlockSpec((1, n), lambda i: (0, 0))
    smem = pl.BlockSpec(memory_space=pltpu.SMEM)
    zcol = lambda w, c: pl.BlockSpec((MIXER_TB, w), lambda i: (i, c))
    kv_col = (2 * SGU_WIDTH + ATTN_WIDTH) // (2 * KV_WIDTH)
    cast_in, cast_out, cast_shapes, holds = _cast_slabs(cast_weights)
    cast_bytes = sum(2 * (4 + 2) * math.prod(s.block_shape[-2:]) for s in cast_in)
    return pl.pallas_call(
        functools.partial(_mixer_kernel, cast_holds=holds),
        out_shape=[jax.ShapeDtypeStruct((SEQ, D_MODEL), BF16)] + cast_shapes,
        grid=(SEQ // MIXER_TB,),
        in_specs=[
            smem, smem,
            zcol(SGU_WIDTH, 0), zcol(SGU_WIDTH, 1), zcol(ATTN_WIDTH, 2),
            zcol(2 * KV_WIDTH, kv_col),
            pl.BlockSpec((BLOCK, 2 * KV_WIDTH),
                         lambda i: (jnp.maximum(MIXER_NB * i - 1, 0), kv_col)),
            row(SGU_WIDTH),
            pl.BlockSpec((None, N_SGU_HEADS, CHUNK, CHUNK), lambda i: (layer, 0, 0, 0)),
            pl.BlockSpec((CHUNK, SGU_WIDTH), lambda i: (0, 0)),
            row(ATTN_WIDTH), row(KV_WIDTH),
            pl.BlockSpec((BLOCK, BAND), lambda i: (0, 0)),
            pl.BlockSpec((SGU_WIDTH, LANES), lambda i: (0, 0)),
            pl.BlockSpec((LANES, SGU_WIDTH), lambda i: (0, 0)),
            row(SGU_WIDTH), row(ATTN_WIDTH),
        ] + cast_in,
        out_specs=[pl.BlockSpec((MIXER_TB, D_MODEL), lambda i: (i, 0))] + cast_out,
        scratch_shapes=[
            pltpu.VMEM((N_LANE_BLOCKS, 2 * CHUNK, CHUNK), BF16),
            pltpu.VMEM((2 * N_KV_HEADS, 2 * BLOCK, 2 * BAND), F32),
            pltpu.VMEM((2, 2 * BLOCK, 2 * BAND), F32),
            pltpu.VMEM((MIXER_NB, N_KV_HEADS, 2 * BAND, LANES), BF16),
            pltpu.VMEM((MIXER_NB, N_KV_HEADS, 2 * BAND, 2 * LANES), BF16),
            pltpu.VMEM((MIXER_NB, 2 * BLOCK, SGU_WIDTH), BF16),
            pltpu.VMEM((MIXER_NB, 4 * BLOCK, LANES), BF16),
        ],
        compiler_params=pltpu.CompilerParams(
            dimension_semantics=("arbitrary",),
            vmem_limit_bytes=MIXER_VMEM_BYTES + cast_bytes),
        name="mixer",
    )(sinks, rel_bias, z, z, z, z, z, sgu_g, sgu_w, sgu_b, qg, kg,
      jnp.asarray(_t5_bucket_table()), jnp.asarray(seg, BF16), jnp.asarray(exp, BF16),
      ga, gb, *[w for w, _ in cast_weights])


def _ffn_kernel(x_ref, mix_ref, wo_ref, g_ref, wg_ref, wu_ref, wd_ref, o_ref, h_ref):
    f = pl.program_id(1)

    @pl.when(f == 0)
    def _():
        x1 = x_ref[...] + _dot(mix_ref[...], wo_ref[...])
        o_ref[...] = x1
        h_ref[...] = _rms_rows(x1, g_ref[...]).astype(BF16)

    h = h_ref[...]
    gate = _dot(h, wg_ref[...])
    up = _dot(h, wu_ref[...])
    act = (gate * jax.nn.sigmoid(gate) * up).astype(BF16)
    o_ref[...] += _dot(act, wd_ref[...])


def _ffn(x, mixed, w_out, g, w_gate, w_up, w_down):
    vmem = (4 * FFN_TM * D_MODEL * 4 + 2 * FFN_TM * D_MODEL * 2
            + 2 * D_MODEL * D_MODEL * 2 + 6 * D_MODEL * FFN_TF * 2
            + FFN_TM * D_MODEL * 2 + 4 * FFN_TM * FFN_TF * 4 + 4 * MIB)
    return pl.pallas_call(
        _ffn_kernel,
        out_shape=jax.ShapeDtypeStruct((SEQ, D_MODEL), F32),
        grid=(SEQ // FFN_TM, D_FF // FFN_TF),
        in_specs=[
            pl.BlockSpec((FFN_TM, D_MODEL), lambda i, f: (i, 0)),
            pl.BlockSpec((FFN_TM, D_MODEL), lambda i, f: (i, 0)),
            pl.BlockSpec((D_MODEL, D_MODEL), lambda i, f: (0, 0)),
            pl.BlockSpec((1, D_MODEL), lambda i, f: (0, 0)),
            pl.BlockSpec((D_MODEL, FFN_TF), lambda i, f: (0, f)),
            pl.BlockSpec((D_MODEL, FFN_TF), lambda i, f: (0, f)),
            pl.BlockSpec((FFN_TF, D_MODEL), lambda i, f: (f, 0)),
        ],
        out_specs=pl.BlockSpec((FFN_TM, D_MODEL), lambda i, f: (i, 0)),
        scratch_shapes=[pltpu.VMEM((FFN_TM, D_MODEL), BF16)],
        compiler_params=pltpu.CompilerParams(
            dimension_semantics=("arbitrary", "arbitrary"),
            vmem_limit_bytes=vmem),
        name="ffn",
    )(x, mixed, w_out, g, w_gate, w_up, w_down)


def kernel(x, rel_bias, norm1_g, w_in, sgu_norm_g, sgu_w, sgu_b, q_norm_g, k_norm_g,
           sinks, out_norm_a, out_norm_b, w_out, norm2_g, w_gate, w_up, w_down):
    assert x.shape == (1, SEQ, D_MODEL)
    h = x.reshape(SEQ, D_MODEL)
    w_in_b = None
    for l in range(DEPTH):
        if l == 0:
            z = _in_proj(h, norm1_g[l].reshape(1, D_MODEL), w_in, layer=0)
        else:
            z = _in_proj(h, norm1_g[l].reshape(1, D_MODEL), w_in_b)
        cast_weights = [(w_out, l), (w_gate, l), (w_up, l), (w_down, l)]
        if l + 1 < DEPTH:
            cast_weights.append((w_in, l + 1))
        mixed, w_out_b, w_gate_b, w_up_b, w_down_b, *w_in_next = _mixer(
            z, sinks[l], rel_bias,
            sgu_norm_g[l].reshape(1, SGU_WIDTH), sgu_w, l,
            jnp.repeat(sgu_b[l].T, HEAD_DIM, axis=1),
            jnp.tile(q_norm_g[l], N_Q_HEADS).reshape(1, ATTN_WIDTH),
            jnp.tile(k_norm_g[l], N_KV_HEADS).reshape(1, KV_WIDTH),
            out_norm_a[l].reshape(1, SGU_WIDTH), out_norm_b[l].reshape(1, ATTN_WIDTH),
            cast_weights)
        if w_in_next:
            w_in_b, = w_in_next
        h = _ffn(h, mixed, w_out_b, norm2_g[l].reshape(1, D_MODEL),
                 w_gate_b, w_up_b, w_down_b)
    return h.reshape(1, SEQ, D_MODEL)
```
